```python
import math
import jax
import jax.numpy as jnp
from jax import lax
import numpy as np

D_MODEL = 1024
BATCH = 4
SEQ = 4096
DEPTH = 4
DEC_BATCH = 32
DEC_SEQ = 64
PAST_LEN = 1024

CHUNK = 64
EPS = 1e-6
N_EVEN = (DEPTH + 1) // 2
N_ODD = DEPTH // 2
GDN_HEADS = 4
GDN_DK = 128
GDN_DV = 128
GDN_CONV = 4
GDN_QKV = GDN_HEADS * (2 * GDN_DK + GDN_DV)
GDN_WIDTH = GDN_HEADS * GDN_DV
SC_WIDTH = D_MODEL - GDN_WIDTH
SC_CONV = 3
EVEN_IN = GDN_QKV + GDN_WIDTH + 2 * GDN_HEADS + 3 * SC_WIDTH
FOX_HEADS = 8
FOX_DH = 128
FOX_WIDTH = FOX_HEADS * FOX_DH
FOX_QBLK = 128
ODD_IN = 3 * FOX_WIDTH + FOX_HEADS
D_FF = 256 * ((8 * D_MODEL + 3 * 256 - 1) // (3 * 256))
INIT_FORGET_BIAS = 3.0
INIT_FORGET_W_SCALE = 0.1

kernel_name = 'hybrid_gdn_shortconv_fox_stream_step'


def rms_norm(x, g):
    xf = x.astype(jnp.float32)
    y = xf * lax.rsqrt(jnp.mean(xf * xf, axis=-1, keepdims=True) + EPS)
    return (y * g.astype(jnp.float32)).astype(x.dtype)


def l2_norm(x):
    xf = x.astype(jnp.float32)
    return xf * lax.rsqrt(jnp.sum(xf * xf, axis=-1, keepdims=True) + EPS)


def causal_dwconv(x, buf, w):
    width = w.shape[0]
    T = x.shape[1]
    xp = jnp.concatenate([buf.astype(x.dtype), x], axis=1)
    y = xp[:, 0:T] * w[0]
    for i in range(1, width):
        y = y + xp[:, i:i + T] * w[i]
    return y, xp[:, T:]


def gated_delta_rule(q, k, v, g, beta, S0, chunk):
    B, T, H, DK = q.shape
    DV = v.shape[-1]
    n = T // chunk

    def blocks(a):
        a = a.reshape((B, n, chunk, H) + a.shape[3:])
        return jnp.moveaxis(a, (1, 3), (0, 2))

    qc, kc, vc = blocks(q), blocks(k), blocks(v)
    gc, bc = blocks(g), blocks(beta)
    G = jnp.cumsum(gc, axis=-1)
    idx = jnp.arange(chunk)
    incl = idx[:, None] >= idx[None, :]
    strict = idx[:, None] > idx[None, :]
    decay = jnp.exp(jnp.where(incl, G[..., :, None] - G[..., None, :], -jnp.inf))
    kb = kc * bc[..., None]
    A = jnp.einsum('nbhid,nbhjd->nbhij', kb, kc) * jnp.where(strict, decay, 0.0)
    rhs = jnp.concatenate([vc * bc[..., None], kb * jnp.exp(G)[..., None]], axis=-1)
    sol = lax.linalg.triangular_solve(A + jnp.eye(chunk, dtype=A.dtype), rhs,
                                      left_side=True, lower=True, unit_diagonal=True)
    u_pre, w = sol[..., :DV], sol[..., DV:]
    P = jnp.einsum('nbhid,nbhjd->nbhij', qc, kc) * decay
    qg = qc * jnp.exp(G)[..., None]
    kd = kc * jnp.exp(G[..., -1:] - G)[..., None]
    gC = jnp.exp(G[..., -1])

    def step(S, xs):
        u_pre_i, w_i, P_i, qg_i, kd_i, gC_i = xs
        u = u_pre_i - jnp.einsum('bhck,bhkv->bhcv', w_i, S)
        o = jnp.einsum('bhck,bhkv->bhcv', qg_i, S) + jnp.einsum('bhij,bhjv->bhiv', P_i, u)
        S = S * gC_i[..., None, None] + jnp.einsum('bhck,bhcv->bhkv', kd_i, u)
        return S, o

    S_fin, o = lax.scan(step, S0, (u_pre, w, P, qg, kd, gC))
    o = jnp.moveaxis(o, (0, 2), (1, 3)).reshape(B, T, H, DV)
    return o, S_fin


def gdn_sconv_mixer(h, conv_buf, S0, sc_buf, w_in, conv_w, a_log, dt_bias, norm_g, sc_w, w_out):
    B, T, _ = h.shape
    p = h @ w_in
    o1 = GDN_QKV
    o2 = o1 + GDN_WIDTH
    o3 = o2 + GDN_HEADS
    o4 = o3 + GDN_HEADS
    o5 = o4 + SC_WIDTH
    o6 = o5 + SC_WIDTH
    qkv, z, a, b = p[..., :o1], p[..., o1:o2], p[..., o2:o3], p[..., o3:o4]
    gate_b, gate_c, h_in = p[..., o4:o5], p[..., o5:o6], p[..., o6:]
    if conv_buf is None:
        conv_buf = jnp.zeros((B, GDN_CONV - 1, GDN_QKV), p.dtype)
    if S0 is None:
        S0 = jnp.zeros((B, GDN_HEADS, GDN_DK, GDN_DV), jnp.float32)
    if sc_buf is None:
        sc_buf = jnp.zeros((B, SC_CONV - 1, SC_WIDTH), p.dtype)
    qkv, conv_buf_new = causal_dwconv(qkv, conv_buf, conv_w)
    qkv = jax.nn.silu(qkv)
    nk = GDN_HEADS * GDN_DK
    q = l2_norm(qkv[..., :nk].reshape(B, T, GDN_HEADS, GDN_DK)) * (GDN_DK ** -0.5)
    k = l2_norm(qkv[..., nk:2 * nk].reshape(B, T, GDN_HEADS, GDN_DK))
    v = qkv[..., 2 * nk:].reshape(B, T, GDN_HEADS, GDN_DV).astype(jnp.float32)
    g = -jnp.exp(a_log.astype(jnp.float32)) * jax.nn.softplus(a.astype(jnp.float32) + dt_bias.astype(jnp.float32))
    beta = jax.nn.sigmoid(b.astype(jnp.float32))
    o, S_new = gated_delta_rule(q, k, v, g, beta, S0.astype(jnp.float32), min(CHUNK, T))
    o = rms_norm(o, norm_g) * jax.nn.silu(z.reshape(B, T, GDN_HEADS, GDN_DV).astype(jnp.float32))
    o = o.reshape(B, T, GDN_WIDTH).astype(h.dtype)
    yc, sc_buf_new = causal_dwconv(gate_c * h_in, sc_buf, sc_w)
    ob = gate_b * yc
    out = jnp.concatenate([o, ob], axis=-1) @ w_out
    return out, conv_buf_new, S_new, sc_buf_new


def fox_project(h, w_in, b_f, qn_g, kn_g):
    B, T, _ = h.shape
    p = h @ w_in
    shp = (B, T, FOX_HEADS, FOX_DH)
    q = rms_norm(p[..., :FOX_WIDTH].reshape(shp), qn_g)
    k = rms_norm(p[..., FOX_WIDTH:2 * FOX_WIDTH].reshape(shp), kn_g)
    v = p[..., 2 * FOX_WIDTH:3 * FOX_WIDTH].reshape(shp)
    logf = jax.nn.log_sigmoid(p[..., 3 * FOX_WIDTH:].astype(jnp.float32) + b_f.astype(jnp.float32))
    return q, k, v, logf


def fox_attend_prompt(q, k, v, logf):
    B, S, H, Dh = q.shape
    nblk = S // FOX_QBLK
    c = jnp.cumsum(logf, axis=1).transpose(0, 2, 1)
    qb = q.reshape(B, nblk, FOX_QBLK, H, Dh).transpose(1, 0, 2, 3, 4)
    cq = c.reshape(B, H, nblk, FOX_QBLK).transpose(2, 0, 1, 3)
    kpos = jnp.arange(S)
    scale = Dh ** -0.5

    def block(args):
        q_i, c_i, start = args
        s = jnp.einsum('bqhd,bkhd->bhqk', q_i, k, preferred_element_type=jnp.float32) * scale
        s = s + c_i[..., :, None] - c[:, :, None, :]
        qpos = start + jnp.arange(FOX_QBLK)
        s = jnp.where(kpos[None, :] <= qpos[:, None], s, -jnp.inf)
        p = jax.nn.softmax(s, axis=-1)
        return jnp.einsum('bhqk,bkhd->bqhd', p.astype(v.dtype), v)

    o = lax.map(block, (qb, cq, jnp.arange(nblk) * FOX_QBLK))
    return o.transpose(1, 0, 2, 3, 4).reshape(B, S, H, Dh)


def fox_attend_sample(q, k, v, logf, k_cache, v_cache, logf_cache):
    B, T, H, Dh = q.shape
    P = k_cache.shape[1]
    k_all = jnp.concatenate([k_cache, k], axis=1)
    v_all = jnp.concatenate([v_cache, v], axis=1)
    lf = jnp.concatenate([logf_cache.astype(jnp.float32), logf], axis=1)
    c = jnp.cumsum(lf, axis=1).transpose(0, 2, 1)
    s = jnp.einsum('bqhd,bkhd->bhqk', q, k_all, preferred_element_type=jnp.float32) * (Dh ** -0.5)
    s = s + c[:, :, P:, None] - c[:, :, None, :]
    qpos = P + jnp.arange(T)
    kpos = jnp.arange(P + T)
    s = jnp.where(kpos[None, :] <= qpos[:, None], s, -jnp.inf)
    p = jax.nn.softmax(s, axis=-1)
    return jnp.einsum('bhqk,bkhd->bqhd', p.astype(v_all.dtype), v_all)


def swiglu(h, w_gate, w_up, w_down):
    return (jax.nn.silu(h @ w_gate) * (h @ w_up)) @ w_down


def trunk(x, gdn_conv, gdn_S, sconv, fox_k, fox_v, fox_logf, pw):
    B, T, _ = x.shape
    has_past = fox_k is not None
    new_k, new_v, new_lf, new_S, new_conv, new_sc = [], [], [], [], [], []
    for layer in range(DEPTH):
        i = layer // 2
        h = rms_norm(x, pw['norm_mix_g'][layer])
        if layer % 2 == 0:
            out, cb, S, sb = gdn_sconv_mixer(
                h,
                gdn_conv[i] if has_past else None,
                gdn_S[i] if has_past else None,
                sconv[i] if has_past else None,
                pw['w_in_even'][i], pw['gdn_conv_w'][i], pw['gdn_a_log'][i], pw['gdn_dt_bias'][i],
                pw['gdn_norm_g'][i], pw['sconv_w'][i], pw['w_out_even'][i])
            new_conv.append(cb)
            new_S.append(S)
            new_sc.append(sb)
        else:
            q, k, v, lf = fox_project(h, pw['w_in_odd'][i], pw['fox_b_f'][i],
                                      pw['fox_q_norm_g'][i], pw['fox_k_norm_g'][i])
            if has_past:
                o = fox_attend_sample(q, k, v, lf, fox_k[i], fox_v[i], fox_logf[i])
            else:
                o = fox_attend_prompt(q, k, v, lf)
            out = o.reshape(B, T, FOX_WIDTH) @ pw['w_out_odd'][i]
            new_k.append(k)
            new_v.append(v)
            new_lf.append(lf)
        x = x + out
        x = x + swiglu(rms_norm(x, pw['norm_ffn_g'][layer]), pw['ffn_w_gate'][layer],
                       pw['ffn_w_up'][layer], pw['ffn_w_down'][layer])
    return (x, jnp.stack(new_k), jnp.stack(new_v), jnp.stack(new_lf),
            jnp.stack(new_S), jnp.stack(new_conv), jnp.stack(new_sc))


def setup_inputs(seed: int = 0) -> dict:
    key = jax.random.key(seed)
    ks = jax.random.split(key, 25)
    f32 = jnp.float32

    def nrm(k, shape, s=1.0):
        return s * jax.random.normal(k, shape, f32)

    x_prompt = nrm(ks[0], (BATCH, SEQ, D_MODEL))
    x_sample = nrm(ks[1], (DEC_BATCH, DEC_SEQ, D_MODEL))
    cache_fox_k = nrm(ks[2], (N_ODD, DEC_BATCH, PAST_LEN, FOX_HEADS, FOX_DH))
    cache_fox_v = nrm(ks[3], (N_ODD, DEC_BATCH, PAST_LEN, FOX_HEADS, FOX_DH))
    cache_fox_logf = jax.nn.log_sigmoid(INIT_FORGET_BIAS + nrm(ks[4], (N_ODD, DEC_BATCH, PAST_LEN, FOX_HEADS), 0.3))
    state_gdn_S = nrm(ks[5], (N_EVEN, DEC_BATCH, GDN_HEADS, GDN_DK, GDN_DV), 0.1)
    state_gdn_conv = nrm(ks[6], (N_EVEN, DEC_BATCH, GDN_CONV - 1, GDN_QKV))
    state_sconv = nrm(ks[7], (N_EVEN, DEC_BATCH, SC_CONV - 1, SC_WIDTH))
    norm_mix_g = 1.0 + nrm(ks[8], (DEPTH, D_MODEL), 0.02)
    norm_ffn_g = 1.0 + nrm(ks[9], (DEPTH, D_MODEL), 0.02)
    w_in_even = nrm(ks[10], (N_EVEN, D_MODEL, EVEN_IN), D_MODEL ** -0.5)
    gdn_conv_w = nrm(ks[11], (N_EVEN, GDN_CONV, GDN_QKV), GDN_CONV ** -0.5)
    gdn_a_log = jnp.log(jax.random.uniform(ks[12], (N_EVEN, GDN_HEADS), f32, 1.0, 16.0))
    dt = jnp.exp(jax.random.uniform(ks[13], (N_EVEN, GDN_HEADS), f32, math.log(1e-3), math.log(1e-1)))
    gdn_dt_bias = dt + jnp.log(-jnp.expm1(-dt))
    gdn_norm_g = 1.0 + nrm(ks[14], (N_EVEN, GDN_DV), 0.02)
    sconv_w = nrm(ks[15], (N_EVEN, SC_CONV, SC_WIDTH), SC_CONV ** -0.5)
    w_out_even = nrm(ks[16], (N_EVEN, D_MODEL, D_MODEL), D_MODEL ** -0.5)
    w_in_odd = nrm(ks[17], (N_ODD, D_MODEL, ODD_IN), D_MODEL ** -0.5)
    w_in_odd = w_in_odd.at[..., 3 * FOX_WIDTH:].multiply(INIT_FORGET_W_SCALE)
    fox_b_f = INIT_FORGET_BIAS + nrm(ks[18], (N_ODD, FOX_HEADS), 0.1)
    fox_q_norm_g = 1.0 + nrm(ks[19], (N_ODD, FOX_DH), 0.02)
    fox_k_norm_g = 1.0 + nrm(ks[20], (N_ODD, FOX_DH), 0.02)
    w_out_odd = nrm(ks[21], (N_ODD, FOX_WIDTH, D_MODEL), FOX_WIDTH ** -0.5)
    ffn_w_gate = nrm(ks[22], (DEPTH, D_MODEL, D_FF), D_MODEL ** -0.5)
    ffn_w_up = nrm(ks[23], (DEPTH, D_MODEL, D_FF), D_MODEL ** -0.5)
    ffn_w_down = nrm(ks[24], (DEPTH, D_FF, D_MODEL), D_FF ** -0.5)
    return {
        'x_prompt': x_prompt, 'x_sample': x_sample,
        'cache_fox_k': cache_fox_k, 'cache_fox_v': cache_fox_v, 'cache_fox_logf': cache_fox_logf,
        'state_gdn_S': state_gdn_S, 'state_gdn_conv': state_gdn_conv, 'state_sconv': state_sconv,
        'norm_mix_g': norm_mix_g, 'norm_ffn_g': norm_ffn_g,
        'w_in_even': w_in_even, 'gdn_conv_w': gdn_conv_w, 'gdn_a_log': gdn_a_log,
        'gdn_dt_bias': gdn_dt_bias, 'gdn_norm_g': gdn_norm_g, 'sconv_w': sconv_w,
        'w_out_even': w_out_even,
        'w_in_odd': w_in_odd, 'fox_b_f': fox_b_f, 'fox_q_norm_g': fox_q_norm_g,
        'fox_k_norm_g': fox_k_norm_g, 'w_out_odd': w_out_odd,
        'ffn_w_gate': ffn_w_gate, 'ffn_w_up': ffn_w_up, 'ffn_w_down': ffn_w_down,
    }


def reference(x_prompt, x_sample, cache_fox_k, cache_fox_v, cache_fox_logf, state_gdn_S,
              state_gdn_conv, state_sconv, norm_mix_g, norm_ffn_g, w_in_even, gdn_conv_w,
              gdn_a_log, gdn_dt_bias, gdn_norm_g, sconv_w, w_out_even, w_in_odd, fox_b_f,
              fox_q_norm_g, fox_k_norm_g, w_out_odd, ffn_w_gate, ffn_w_up, ffn_w_down):
    pw = dict(norm_mix_g=norm_mix_g, norm_ffn_g=norm_ffn_g, w_in_even=w_in_even,
              gdn_conv_w=gdn_conv_w, gdn_a_log=gdn_a_log, gdn_dt_bias=gdn_dt_bias,
              gdn_norm_g=gdn_norm_g, sconv_w=sconv_w, w_out_even=w_out_even, w_in_odd=w_in_odd,
              fox_b_f=fox_b_f, fox_q_norm_g=fox_q_norm_g, fox_k_norm_g=fox_k_norm_g,
              w_out_odd=w_out_odd, ffn_w_gate=ffn_w_gate, ffn_w_up=ffn_w_up, ffn_w_down=ffn_w_down)
    (y_prompt, p_fox_k, p_fox_v, p_fox_logf,
     p_gdn_S, p_gdn_conv, p_sconv) = trunk(x_prompt, None, None, None, None, None, None, pw)
    (y_sample, s_fox_k, s_fox_v, s_fox_logf,
     s_gdn_S, s_gdn_conv, s_sconv) = trunk(x_sample, state_gdn_conv, state_gdn_S, state_sconv,
                                           cache_fox_k, cache_fox_v, cache_fox_logf, pw)
    return (y_prompt, y_sample, p_fox_k, p_fox_v, p_fox_logf, p_gdn_S, p_gdn_conv, p_sconv,
            s_fox_k, s_fox_v, s_fox_logf, s_gdn_S, s_gdn_conv, s_sconv)
```

```python
import functools

import jax
import jax.numpy as jnp
from jax import lax
from jax.experimental import pallas as pl
from jax.experimental.pallas import tpu as pltpu

F32 = jnp.float32
BF16 = jnp.bfloat16

EPS = 1e-6
CHUNK = 64
GDN_HEADS = 4
GDN_DK = 128
GDN_DV = 128
GDN_CONV = 4
SC_CONV = 3
FOX_HEADS = 8
FOX_DH = 128
LANES = 128
SUBLANES = 8
INV_BLOCK = 16
LOG2E = 1.4426950408889634
NEG_BIG = -1e30
VMEM_LIMIT_BYTES = 56 * 1024 * 1024

TOKEN_TILE = 512
ATTN_TILE = 512


def _params(semantics):
    return pltpu.CompilerParams(dimension_semantics=semantics,
                                vmem_limit_bytes=VMEM_LIMIT_BYTES)


def _mm(a, b):
    return jnp.dot(a.astype(BF16), b.astype(BF16), preferred_element_type=F32)


def _mm_nt(a, b):
    return lax.dot_general(a.astype(BF16), b.astype(BF16), (((1,), (1,)), ((), ())),
                           preferred_element_type=F32)


def _mm_tn(a, b):
    return lax.dot_general(a.astype(BF16), b.astype(BF16), (((0,), (0,)), ((), ())),
                           preferred_element_type=F32)


def _split3(x):
    hi = x.astype(BF16)
    r1 = x - hi.astype(F32)
    mid = r1.astype(BF16)
    lo = (r1 - mid.astype(F32)).astype(BF16)
    return hi, mid, lo


def _rms_norm(x, g):
    return x * lax.rsqrt(jnp.mean(x * x, axis=-1, keepdims=True) + EPS) * g


def _softplus(x):
    return jnp.maximum(x, 0.0) + jnp.log1p(jnp.exp(-jnp.abs(x)))


def _sigmoid(x):
    return 1.0 / (1.0 + jnp.exp(-x))


def _silu(x):
    return x * _sigmoid(x)


def _in_even_kernel(x_ref, g_ref, wqkv_ref, wz_ref, wab_ref, wgb_ref, wgc_ref, whin_ref,
                    qkv_ref, z_ref, ab_ref, gb_ref, u_ref):
    h = _rms_norm(x_ref[...], g_ref[...]).astype(BF16)
    qkv_ref[...] = jnp.dot(h, wqkv_ref[...], preferred_element_type=F32)
    z_ref[...] = jnp.dot(h, wz_ref[...], preferred_element_type=F32).astype(BF16)
    ab_ref[...] = jnp.dot(h, wab_ref[...], preferred_element_type=F32)
    gb_ref[...] = jnp.dot(h, wgb_ref[...], preferred_element_type=F32).astype(BF16)
    gate_c = jnp.dot(h, wgc_ref[...], preferred_element_type=F32)
    h_in = jnp.dot(h, whin_ref[...], preferred_element_type=F32)
    u_ref[...] = gate_c * h_in


def _in_even(x, g, wqkv, wz, wab, wgb, wgc, whin):
    t, d = x.shape
    tm = min(TOKEN_TILE, t)
    nqkv, nz, nsc = wqkv.shape[1], wz.shape[1], wgb.shape[1]
    row = lambda n: pl.BlockSpec((tm, n), lambda i: (i, 0))
    full = lambda a: pl.BlockSpec(a.shape, lambda i: (0, 0))
    return pl.pallas_call(
        _in_even_kernel,
        grid=(t // tm,),
        in_specs=[row(d), full(g), full(wqkv), full(wz), full(wab), full(wgb), full(wgc),
                  full(whin)],
        out_specs=[row(nqkv), row(nz), row(LANES), row(nsc), row(nsc)],
        out_shape=[jax.ShapeDtypeStruct((t, nqkv), F32),
                   jax.ShapeDtypeStruct((t, nz), BF16),
                   jax.ShapeDtypeStruct((t, LANES), F32),
                   jax.ShapeDtypeStruct((t, nsc), BF16),
                   jax.ShapeDtypeStruct((t, nsc), F32)],
        compiler_params=_params(("parallel",)),
        name="in_even",
    )(x, g, wqkv, wz, wab, wgb, wgc, whin)


def _unit_lower_inverse(a_strict):
    c = a_strict.shape[0]
    ii = lax.broadcasted_iota(jnp.int32, (c, c), 0)
    jj = lax.broadcasted_iota(jnp.int32, (c, c), 1)
    eye = (ii == jj).astype(F32)
    same_block = (ii // INV_BLOCK) == (jj // INV_BLOCK)
    a0 = jnp.where(same_block, a_strict, 0.0)
    a1 = a_strict - a0
    d_inv = eye - a0
    power = a0
    steps = INV_BLOCK.bit_length() - 2
    for _ in range(steps):
        power = _mm(power, power)
        d_inv = d_inv + _mm(d_inv, power)
    n = _mm(d_inv, a1)
    m = eye - n
    power = n
    nblk = c // INV_BLOCK
    for _ in range(max(nblk.bit_length() - 2, 0)):
        power = _mm(power, power)
        m = m + _mm(m, power)
    return _mm(m, d_inv)


def _gdn_kernel(qkv_ref, z_ref, ab_ref, gb_ref, u_ref, conv0_ref, s0_ref, sc0_ref,
                convw_ref, alog_ref, dtb_ref, ng_ref, scw_ref,
                out_ref, s_ref, conv_ref, sc_ref, xbuf, ubuf):
    c = pl.program_id(1)
    ch = CHUNK
    top = SUBLANES
    kc, ks = GDN_CONV - 1, SC_CONV - 1

    @pl.when(c == 0)
    def _():
        xbuf[top - kc:top, :] = conv0_ref[0]
        ubuf[top - ks:top, :] = sc0_ref[0]
        s_ref[...] = s0_ref[...]

    @pl.when(c > 0)
    def _():
        xbuf[top - kc:top, :] = xbuf[top + ch - kc:top + ch, :]
        ubuf[top - ks:top, :] = ubuf[top + ch - ks:top + ch, :]

    xbuf[top:top + ch, :] = qkv_ref[...]
    ubuf[top:top + ch, :] = u_ref[...]
    conv_ref[0] = xbuf[top + ch - kc:top + ch, :]
    sc_ref[0] = ubuf[top + ch - ks:top + ch, :]

    cw = convw_ref[...]
    y = xbuf[top - kc:top - kc + ch, :] * cw[0:1, :]
    for i in range(1, GDN_CONV):
        y = y + xbuf[top - kc + i:top - kc + i + ch, :] * cw[i:i + 1, :]
    y = _silu(y)

    ab = ab_ref[...]
    g = -jnp.exp(alog_ref[...]) * _softplus(ab + dtb_ref[...])
    beta = _sigmoid(ab)

    ii = lax.broadcasted_iota(jnp.int32, (ch, ch), 0)
    jj = lax.broadcasted_iota(jnp.int32, (ch, ch), 1)
    incl = ii >= jj
    strict = ii > jj
    tri = incl.astype(BF16)
    g_hi, g_mid, g_lo = _split3(g)
    gcum = (jnp.dot(tri, g_hi, preferred_element_type=F32)
            + jnp.dot(tri, g_mid, preferred_element_type=F32)
            + jnp.dot(tri, g_lo, preferred_element_type=F32))
    gcum_t = gcum.T

    nk = GDN_HEADS * GDN_DK
    z = z_ref[...].astype(F32)
    ng = ng_ref[...]
    for h in range(GDN_HEADS):
        qh = y[:, h * GDN_DK:(h + 1) * GDN_DK]
        kh = y[:, nk + h * GDN_DK:nk + (h + 1) * GDN_DK]
        vh = y[:, 2 * nk + h * GDN_DV:2 * nk + (h + 1) * GDN_DV]
        qh = qh * lax.rsqrt(jnp.sum(qh * qh, axis=-1, keepdims=True) + EPS) * (GDN_DK ** -0.5)
        kh = kh * lax.rsqrt(jnp.sum(kh * kh, axis=-1, keepdims=True) + EPS)
        gc = gcum[:, h:h + 1]
        gr = gcum_t[h:h + 1, :]
        bh = beta[:, GDN_HEADS + h:GDN_HEADS + h + 1]
        decay = jnp.exp(jnp.where(incl, gc - gr, -jnp.inf))
        kb = kh * bh
        a = _mm_nt(kb, kh) * jnp.where(strict, decay, 0.0)
        eg = jnp.exp(gc)
        rhs = jnp.concatenate([vh * bh, kb * eg], axis=-1)
        sol = _mm(_unit_lower_inverse(a), rhs)
        u_pre, w = sol[:, :GDN_DV], sol[:, GDN_DV:]
        p = _mm_nt(qh, kh) * decay
        g_last = gcum[ch - 1:ch, h:h + 1]
        s = s_ref[0, h]
        u = u_pre - _mm(w, s)
        o = _mm(qh * eg, s) + _mm(p, u)
        s_ref[0, h] = s * jnp.exp(g_last) + _mm_tn(kh * jnp.exp(g_last - gc), u)
        zh = z[:, h * GDN_DV:(h + 1) * GDN_DV]
        o = _rms_norm(o, ng) * _silu(zh)
        out_ref[:, h * GDN_DV:(h + 1) * GDN_DV] = o.astype(out_ref.dtype)

    sw = scw_ref[...]
    yc = ubuf[top - ks:top - ks + ch, :] * sw[0:1, :]
    for i in range(1, SC_CONV):
        yc = yc + ubuf[top - ks + i:top - ks + i + ch, :] * sw[i:i + 1, :]
    ob = gb_ref[...].astype(F32) * yc
    wa = GDN_HEADS * GDN_DV
    out_ref[:, wa:] = ob.astype(out_ref.dtype)


def _gdn_sconv(qkv, z, ab, gb, u, conv0, s0, sc0, convw, alog, dtb, ng, scw, nseq):
    t = qkv.shape[0]
    nchunk = t // (nseq * CHUNK)
    nqkv, nz, nsc = qkv.shape[1], z.shape[1], u.shape[1]
    row = lambda n: pl.BlockSpec((CHUNK, n), lambda b, c: (b * nchunk + c, 0))
    seq3 = lambda a: pl.BlockSpec((1,) + a.shape[1:], lambda b, c: (b, 0, 0))
    seq4 = lambda a: pl.BlockSpec((1,) + a.shape[1:], lambda b, c: (b, 0, 0, 0))
    full = lambda a: pl.BlockSpec(a.shape, lambda b, c: (0, 0))
    return pl.pallas_call(
        _gdn_kernel,
        grid=(nseq, nchunk),
        in_specs=[row(nqkv), row(nz), row(LANES), row(nsc), row(nsc),
                  seq3(conv0), seq4(s0), seq3(sc0),
                  full(convw), full(alog), full(dtb), full(ng), full(scw)],
        out_specs=[row(nz + nsc), seq4(s0), seq3(conv0), seq3(sc0)],
        out_shape=[jax.ShapeDtypeStruct((t, nz + nsc), BF16),
                   jax.ShapeDtypeStruct(s0.shape, F32),
                   jax.ShapeDtypeStruct(conv0.shape, F32),
                   jax.ShapeDtypeStruct(sc0.shape, F32)],
        scratch_shapes=[pltpu.VMEM((SUBLANES + CHUNK, nqkv), F32),
                        pltpu.VMEM((SUBLANES + CHUNK, nsc), F32)],
        compiler_params=_params(("parallel", "arbitrary")),
        name="gdn_sconv",
    )(qkv, z, ab, gb, u, conv0, s0, sc0, convw, alog, dtb, ng, scw)


def _out_ffn_kernel(x_ref, mix_ref, wout_ref, g_ref, wg_ref, wu_ref, wd_ref, o_ref,
                    xnew, hbuf, acc):
    j = pl.program_id(1)

    @pl.when(j == 0)
    def _():
        xn = x_ref[...] + jnp.dot(mix_ref[...], wout_ref[...], preferred_element_type=F32)
        xnew[...] = xn
        hbuf[...] = _rms_norm(xn, g_ref[...]).astype(BF16)
        acc[...] = jnp.zeros_like(acc)

    h = hbuf[...]
    gate = jnp.dot(h, wg_ref[...], preferred_element_type=F32)
    up = jnp.dot(h, wu_ref[...], preferred_element_type=F32)
    act = (_silu(gate) * up).astype(BF16)
    acc[...] += jnp.dot(act, wd_ref[...], preferred_element_type=F32)

    @pl.when(j == pl.num_programs(1) - 1)
    def _():
        o_ref[...] = xnew[...] + acc[...]


def _ffn_tile(dff):
    best = LANES
    for cand in range(LANES, dff // 2 + 1, LANES):
        if dff % cand == 0:
            best = cand
    return best


def _out_ffn(x, mix, wout, g, wg, wu, wd):
    t, d = x.shape
    dff = wg.shape[1]
    tm = min(TOKEN_TILE, t)
    tf = _ffn_tile(dff)
    return pl.pallas_call(
        _out_ffn_kernel,
        grid=(t // tm, dff // tf),
        in_specs=[pl.BlockSpec((tm, d), lambda i, j: (i, 0)),
                  pl.BlockSpec((tm, mix.shape[1]), lambda i, j: (i, 0)),
                  pl.BlockSpec(wout.shape, lambda i, j: (0, 0)),
                  pl.BlockSpec(g.shape, lambda i, j: (0, 0)),
                  pl.BlockSpec((d, tf), lambda i, j: (0, j)),
                  pl.BlockSpec((d, tf), lambda i, j: (0, j)),
                  pl.BlockSpec((tf, d), lambda i, j: (j, 0))],
        out_specs=pl.BlockSpec((tm, d), lambda i, j: (i, 0)),
        out_shape=jax.ShapeDtypeStruct((t, d), F32),
        scratch_shapes=[pltpu.VMEM((tm, d), F32), pltpu.VMEM((tm, d), BF16),
                        pltpu.VMEM((tm, d), F32)],
        compiler_params=_params(("parallel", "arbitrary")),
        name="out_ffn",
    )(x, mix, wout, g, wg, wu, wd)


def _in_odd_kernel(x_ref, g_ref, wq_ref, wk_ref, wv_ref, wf_ref, bf_ref, qg_ref, kg_ref,
                   q_ref, k_ref, kb_ref, v_ref, vb_ref, lf_ref):
    h = _rms_norm(x_ref[...], g_ref[...]).astype(BF16)
    q = jnp.dot(h, wq_ref[...], preferred_element_type=F32)
    k = jnp.dot(h, wk_ref[...], preferred_element_type=F32)
    v = jnp.dot(h, wv_ref[...], preferred_element_type=F32)
    qg, kg = qg_ref[...], kg_ref[...]
    qscale = (FOX_DH ** -0.5) * LOG2E
    for hd in range(FOX_HEADS):
        sl = slice(hd * FOX_DH, (hd + 1) * FOX_DH)
        qn = _rms_norm(q[:, sl], qg)
        kn = _rms_norm(k[:, sl], kg)
        q_ref[:, sl] = (qn * qscale).astype(BF16)
        k_ref[:, sl] = kn
        kb_ref[:, sl] = kn.astype(BF16)
    v_ref[...] = v
    vb_ref[...] = v.astype(BF16)
    f = jnp.dot(h, wf_ref[...], preferred_element_type=F32) + bf_ref[...]
    lf_ref[...] = (-_softplus(-f))[:, :FOX_HEADS]


def _in_odd(x, g, wq, wk, wv, wf, bf, qg, kg):
    t, d = x.shape
    tm = min(TOKEN_TILE, t)
    w = wq.shape[1]
    row = lambda n: pl.BlockSpec((tm, n), lambda i: (i, 0))
    full = lambda a: pl.BlockSpec(a.shape, lambda i: (0, 0))
    return pl.pallas_call(
        _in_odd_kernel,
        grid=(t // tm,),
        in_specs=[row(d), full(g), full(wq), full(wk), full(wv), full(wf), full(bf),
                  full(qg), full(kg)],
        out_specs=[row(w), row(w), row(w), row(w), row(w), row(FOX_HEADS)],
        out_shape=[jax.ShapeDtypeStruct((t, w), BF16),
                   jax.ShapeDtypeStruct((t, w), F32),
                   jax.ShapeDtypeStruct((t, w), BF16),
                   jax.ShapeDtypeStruct((t, w), F32),
                   jax.ShapeDtypeStruct((t, w), BF16),
                   jax.ShapeDtypeStruct((t, FOX_HEADS), F32)],
        compiler_params=_params(("parallel",)),
        name="in_odd",
    )(x, g, wq, wk, wv, wf, bf, qg, kg)


def _key_bias_kernel(lf_ref, o_ref, carry):
    j = pl.program_id(1)

    @pl.when(j == 0)
    def _():
        carry[...] = jnp.zeros_like(carry)

    lf = lf_ref[0]
    tt = lf.shape[1]
    upper = (lax.broadcasted_iota(jnp.int32, (tt, tt), 0)
             <= lax.broadcasted_iota(jnp.int32, (tt, tt), 1)).astype(BF16)
    hi, mid, lo = _split3(lf)
    csum = (jnp.dot(hi, upper, preferred_element_type=F32)
            + jnp.dot(mid, upper, preferred_element_type=F32)
            + jnp.dot(lo, upper, preferred_element_type=F32)) + carry[:, 0:1]
    o_ref[0] = csum * (-LOG2E)
    carry[...] = jnp.broadcast_to(csum[:, tt - 1:tt], carry.shape)


def _key_bias(lf_t, tile):
    b, h, s = lf_t.shape
    tt = tile if s % tile == 0 else s
    return pl.pallas_call(
        _key_bias_kernel,
        grid=(b, s // tt),
        in_specs=[pl.BlockSpec((1, h, tt), lambda i, j: (i, 0, j))],
        out_specs=pl.BlockSpec((1, h, tt), lambda i, j: (i, 0, j)),
        out_shape=jax.ShapeDtypeStruct((b, h, s), F32),
        scratch_shapes=[pltpu.VMEM((h, LANES), F32)],
        compiler_params=_params(("parallel", "arbitrary")),
        name="key_bias",
    )(lf_t)


def _online_softmax_step(s, vh, m_ref, l_ref, acc_ref, h):
    m_prev = m_ref[h][:, 0:1]
    m_new = jnp.maximum(m_prev, jnp.max(s, axis=-1, keepdims=True))
    alpha = jnp.exp2(m_prev - m_new)
    p = jnp.exp2(s - m_new)
    l_ref[h] = alpha * l_ref[h] + jnp.sum(p, axis=-1, keepdims=True)
    acc_ref[h] = alpha * acc_ref[h] + jnp.dot(p.astype(BF16), vh, preferred_element_type=F32)
    m_ref[h] = jnp.broadcast_to(m_new, m_ref.shape[1:])


def _attn_prompt_kernel(q_ref, k_ref, v_ref, nb_ref, o_ref, m_ref, l_ref, acc_ref):
    i, j = pl.program_id(1), pl.program_id(2)
    tq, tk = q_ref.shape[0], k_ref.shape[0]

    @pl.when(j == 0)
    def _():
        m_ref[...] = jnp.full_like(m_ref, NEG_BIG)
        l_ref[...] = jnp.zeros_like(l_ref)
        acc_ref[...] = jnp.zeros_like(acc_ref)

    def step(masked):
        nb = nb_ref[0]
        if masked:
            causal = (lax.broadcasted_iota(jnp.int32, (tq, tk), 0)
                      >= lax.broadcasted_iota(jnp.int32, (tq, tk), 1))
        for h in range(FOX_HEADS):
            sl = slice(h * FOX_DH, (h + 1) * FOX_DH)
            s = lax.dot_general(q_ref[:, sl], k_ref[:, sl], (((1,), (1,)), ((), ())),
                                preferred_element_type=F32) + nb[h:h + 1, :]
            if masked:
                s = jnp.where(causal, s, NEG_BIG)
            _online_softmax_step(s, v_ref[:, sl], m_ref, l_ref, acc_ref, h)

    @pl.when(j < i)
    def _():
        step(False)

    @pl.when(j == i)
    def _():
        step(True)
        for h in range(FOX_HEADS):
            sl = slice(h * FOX_DH, (h + 1) * FOX_DH)
            o_ref[:, sl] = (acc_ref[h] / l_ref[h]).astype(o_ref.dtype)


def _attn_prompt(q, k, v, nbias, nseq):
    t, w = q.shape
    s = t // nseq
    tile = min(ATTN_TILE, s)
    nblk = s // tile
    kv_spec = pl.BlockSpec((tile, w), lambda b, i, j: (b * nblk + jnp.minimum(j, i), 0))
    return pl.pallas_call(
        _attn_prompt_kernel,
        grid=(nseq, nblk, nblk),
        in_specs=[pl.BlockSpec((tile, w), lambda b, i, j: (b * nblk + i, 0)),
                  kv_spec, kv_spec,
                  pl.BlockSpec((1, FOX_HEADS, tile), lambda b, i, j: (b, 0, jnp.minimum(j, i)))],
        out_specs=pl.BlockSpec((tile, w), lambda b, i, j: (b * nblk + i, 0)),
        out_shape=jax.ShapeDtypeStruct((t, w), BF16),
        scratch_shapes=[pltpu.VMEM((FOX_HEADS, tile, LANES), F32),
                        pltpu.VMEM((FOX_HEADS, tile, LANES), F32),
                        pltpu.VMEM((FOX_HEADS, tile, FOX_DH), F32)],
        compiler_params=_params(("parallel", "parallel", "arbitrary")),
        name="attn_prompt",
    )(q, k, v, nbias)


def _attn_sample_kernel(q_ref, kc_ref, vc_ref, kn_ref, vn_ref, nb_ref, o_ref):
    tq = q_ref.shape[0]
    past = kc_ref.shape[1]
    nb = nb_ref[0]
    causal = (lax.broadcasted_iota(jnp.int32, (tq, tq), 0)
              >= lax.broadcasted_iota(jnp.int32, (tq, tq), 1))
    for h in range(FOX_HEADS):
        sl = slice(h * FOX_DH, (h + 1) * FOX_DH)
        qh = q_ref[:, sl]
        kc = kc_ref[0, :, sl].astype(BF16)
        vc = vc_ref[0, :, sl].astype(BF16)
        s_c = lax.dot_general(qh, kc, (((1,), (1,)), ((), ())),
                              preferred_element_type=F32) + nb[h:h + 1, :past]
        s_n = lax.dot_general(qh, kn_ref[:, sl], (((1,), (1,)), ((), ())),
                              preferred_element_type=F32) + nb[h:h + 1, past:]
        s_n = jnp.where(causal, s_n, NEG_BIG)
        m = jnp.maximum(jnp.max(s_c, axis=-1, keepdims=True),
                        jnp.max(s_n, axis=-1, keepdims=True))
        p_c = jnp.exp2(s_c - m)
        p_n = jnp.exp2(s_n - m)
        l = jnp.sum(p_c, axis=-1, keepdims=True) + jnp.sum(p_n, axis=-1, keepdims=True)
        o = (jnp.dot(p_c.astype(BF16), vc, preferred_element_type=F32)
             + jnp.dot(p_n.astype(BF16), vn_ref[:, sl], preferred_element_type=F32))
        o_ref[:, sl] = (o / l).astype(o_ref.dtype)


def _attn_sample(q, k_new, v_new, k_cache, v_cache, nbias, nseq):
    t, w = q.shape
    tq = t // nseq
    past = k_cache.shape[1]
    row = pl.BlockSpec((tq, w), lambda b: (b, 0))
    cache = pl.BlockSpec((1, past, w), lambda b: (b, 0, 0))
    return pl.pallas_call(
        _attn_sample_kernel,
        grid=(nseq,),
        in_specs=[row, cache, cache, row, row,
                  pl.BlockSpec((1, FOX_HEADS, past + tq), lambda b: (b, 0, 0))],
        out_specs=row,
        out_shape=jax.ShapeDtypeStruct((t, w), BF16),
        compiler_params=_params(("parallel",)),
        name="attn_sample",
    )(q, k_cache, v_cache, k_new, v_new, nbias)


def _pad_lanes(w):
    return jnp.pad(w, ((0, 0), (0, LANES - w.shape[1])))


def _row(v, width=None):
    v = v.reshape(1, -1).astype(F32)
    if width is not None:
        v = jnp.pad(v, ((0, 0), (0, width - v.shape[1])))
    return v


def _prep_weights(norm_mix_g, norm_ffn_g, w_in_even, gdn_conv_w, gdn_a_log, gdn_dt_bias,
                  gdn_norm_g, sconv_w, w_out_even, w_in_odd, fox_b_f, fox_q_norm_g,
                  fox_k_norm_g, w_out_odd, ffn_w_gate, ffn_w_up, ffn_w_down):
    depth = norm_mix_g.shape[0]
    nqkv = gdn_conv_w.shape[2]
    nz = GDN_HEADS * GDN_DV
    nsc = sconv_w.shape[2]
    o1, o2 = nqkv, nqkv + nz
    o4 = o2 + 2 * GDN_HEADS
    o5, o6 = o4 + nsc, o4 + 2 * nsc
    fw = FOX_HEADS * FOX_DH
    layers = []
    for layer in range(depth):
        i = layer // 2
        p = dict(g_mix=_row(norm_mix_g[layer]), g_ffn=_row(norm_ffn_g[layer]),
                 wg=ffn_w_gate[layer].astype(BF16), wu=ffn_w_up[layer].astype(BF16),
                 wd=ffn_w_down[layer].astype(BF16))
        if layer % 2 == 0:
            w = w_in_even[i]
            p.update(wqkv=w[:, :o1].astype(BF16), wz=w[:, o1:o2].astype(BF16),
                     wab=_pad_lanes(w[:, o2:o4]).astype(BF16),
                     wgb=w[:, o4:o5].astype(BF16), wgc=w[:, o5:o6].astype(BF16),
                     whin=w[:, o6:].astype(BF16),
                     convw=gdn_conv_w[i].astype(F32), alog=_row(gdn_a_log[i], LANES),
                     dtb=_row(gdn_dt_bias[i], LANES), ng=_row(gdn_norm_g[i]),
                     scw=sconv_w[i].astype(F32), wout=w_out_even[i].astype(BF16))
        else:
            w = w_in_odd[i]
            p.update(wq=w[:, :fw].astype(BF16), wk=w[:, fw:2 * fw].astype(BF16),
                     wv=w[:, 2 * fw:3 * fw].astype(BF16),
                     wf=_pad_lanes(w[:, 3 * fw:]).astype(BF16),
                     bf=_row(fox_b_f[i], LANES), qg=_row(fox_q_norm_g[i]),
                     kg=_row(fox_k_norm_g[i]), wout=w_out_odd[i].astype(BF16))
        layers.append(p)
    return layers


def _trunk(x, layers, gdn_conv, gdn_s, sconv, fox_k, fox_v, fox_logf):
    nseq, seq, d = x.shape
    x = x.reshape(nseq * seq, d)
    has_past = fox_k is not None
    new_k, new_v, new_lf, new_s, new_conv, new_sc = [], [], [], [], [], []
    for layer, p in enumerate(layers):
        i = layer // 2
        if layer % 2 == 0:
            qkv, z, ab, gb, u = _in_even(x, p["g_mix"], p["wqkv"], p["wz"], p["wab"],
                                         p["wgb"], p["wgc"], p["whin"])
            if has_past:
                conv0, s0, sc0 = gdn_conv[i], gdn_s[i], sconv[i]
            else:
                conv0 = jnp.zeros((nseq, GDN_CONV - 1, qkv.shape[1]), F32)
                s0 = jnp.zeros((nseq, GDN_HEADS, GDN_DK, GDN_DV), F32)
                sc0 = jnp.zeros((nseq, SC_CONV - 1, u.shape[1]), F32)
            mix, s_fin, conv_fin, sc_fin = _gdn_sconv(
                qkv, z, ab, gb, u, conv0, s0, sc0, p["convw"], p["alog"], p["dtb"], p["ng"],
                p["scw"], nseq)
            new_s.append(s_fin)
            new_conv.append(conv_fin)
            new_sc.append(sc_fin)
        else:
            q, k, kb, v, vb, lf = _in_odd(x, p["g_mix"], p["wq"], p["wk"], p["wv"], p["wf"],
                                          p["bf"], p["qg"], p["kg"])
            lf = lf.reshape(nseq, seq, FOX_HEADS)
            if has_past:
                past = fox_k.shape[2]
                lf_all = jnp.concatenate([fox_logf[i].astype(F32), lf], axis=1)
                nbias = _key_bias(lf_all.transpose(0, 2, 1), ATTN_TILE)
                mix = _attn_sample(q, kb, vb, fox_k[i].reshape(nseq, past, -1),
                                   fox_v[i].reshape(nseq, past, -1), nbias, nseq)
            else:
                nbias = _key_bias(lf.transpose(0, 2, 1), ATTN_TILE)
                mix = _attn_prompt(q, kb, vb, nbias, nseq)
            new_k.append(k.reshape(nseq, seq, FOX_HEADS, FOX_DH))
            new_v.append(v.reshape(nseq, seq, FOX_HEADS, FOX_DH))
            new_lf.append(lf)
        x = _out_ffn(x, mix, p["wout"], p["g_ffn"], p["wg"], p["wu"], p["wd"])
    return (x.reshape(nseq, seq, d), jnp.stack(new_k), jnp.stack(new_v), jnp.stack(new_lf),
            jnp.stack(new_s), jnp.stack(new_conv), jnp.stack(new_sc))


def kernel(x_prompt, x_sample, cache_fox_k, cache_fox_v, cache_fox_logf, state_gdn_S,
           state_gdn_conv, state_sconv, norm_mix_g, norm_ffn_g, w_in_even, gdn_conv_w,
           gdn_a_log, gdn_dt_bias, gdn_norm_g, sconv_w, w_out_even, w_in_odd, fox_b_f,
           fox_q_norm_g, fox_k_norm_g, w_out_odd, ffn_w_gate, ffn_w_up, ffn_w_down):
    layers = _prep_weights(norm_mix_g, norm_ffn_g, w_in_even, gdn_conv_w, gdn_a_log,
                           gdn_dt_bias, gdn_norm_g, sconv_w, w_out_even, w_in_odd, fox_b_f,
                           fox_q_norm_g, fox_k_norm_g, w_out_odd, ffn_w_gate, ffn_w_up,
                           ffn_w_down)
    (y_prompt, p_fox_k, p_fox_v, p_fox_logf, p_gdn_s, p_gdn_conv, p_sconv) = _trunk(
        x_prompt, layers, None, None, None, None, None, None)
    (y_sample, s_fox_k, s_fox_v, s_fox_logf, s_gdn_s, s_gdn_conv, s_sconv) = _trunk(
        x_sample, layers, state_gdn_conv, state_gdn_S, state_sconv, cache_fox_k, cache_fox_v,
        cache_fox_logf)
    return (y_prompt, y_sample, p_fox_k, p_fox_v, p_fox_logf, p_gdn_s, p_gdn_conv, p_sconv,
            s_fox_k, s_fox_v, s_fox_logf, s_gdn_s, s_gdn_conv, s_sconv)
```

```python
import functools

import jax
import jax.numpy as jnp
from jax import lax
from jax.experimental import pallas as pl
from jax.experimental.pallas import tpu as pltpu

F32 = jnp.float32
BF16 = jnp.bfloat16

EPS = 1e-6
CHUNK = 64
GDN_HEADS = 4
GDN_DK = 128
GDN_DV = 128
GDN_CONV = 4
SC_CONV = 3
FOX_HEADS = 8
FOX_DH = 128
LANES = 128
SUBLANES = 8
INV_BLOCK = 16
LOG2E = 1.4426950408889634
NEG_BIG = -1e30
VMEM_LIMIT_BYTES = 56 * 1024 * 1024

TOKEN_TILE = 512
ATTN_TILE = 512
GDN_BLOCKS_PER_STEP = 8


def _params(semantics):
    return pltpu.CompilerParams(dimension_semantics=semantics,
                                vmem_limit_bytes=VMEM_LIMIT_BYTES)


def _mm(a, b):
    return jnp.dot(a.astype(BF16), b.astype(BF16), preferred_element_type=F32)


def _mm_nt(a, b):
    return lax.dot_general(a.astype(BF16), b.astype(BF16), (((1,), (1,)), ((), ())),
                           preferred_element_type=F32)


def _split3(x):
    hi = x.astype(BF16)
    r1 = x - hi.astype(F32)
    mid = r1.astype(BF16)
    lo = (r1 - mid.astype(F32)).astype(BF16)
    return hi, mid, lo


def _rms_norm(x, g):
    return x * lax.rsqrt(jnp.mean(x * x, axis=-1, keepdims=True) + EPS) * g


def _softplus(x):
    return jnp.maximum(x, 0.0) + jnp.log1p(jnp.exp(-jnp.abs(x)))


def _sigmoid(x):
    return 1.0 / (1.0 + jnp.exp(-x))


def _silu(x):
    return x * _sigmoid(x)


def _in_even_kernel(x_ref, g_ref, wqkv_ref, wz_ref, wab_ref, wgb_ref, wgc_ref, whin_ref,
                    qkv_ref, z_ref, ab_ref, gb_ref, u_ref):
    h = _rms_norm(x_ref[...], g_ref[...]).astype(BF16)
    qkv_ref[...] = jnp.dot(h, wqkv_ref[...], preferred_element_type=F32)
    z_ref[...] = jnp.dot(h, wz_ref[...], preferred_element_type=F32).astype(BF16)
    ab_ref[...] = jnp.dot(h, wab_ref[...], preferred_element_type=F32)
    gb_ref[...] = jnp.dot(h, wgb_ref[...], preferred_element_type=F32).astype(BF16)
    gate_c = jnp.dot(h, wgc_ref[...], preferred_element_type=F32)
    h_in = jnp.dot(h, whin_ref[...], preferred_element_type=F32)
    u_ref[...] = gate_c * h_in


def _in_even(x, g, wqkv, wz, wab, wgb, wgc, whin):
    t, d = x.shape
    tm = min(TOKEN_TILE, t)
    nqkv, nz, nsc = wqkv.shape[1], wz.shape[1], wgb.shape[1]
    row = lambda n: pl.BlockSpec((tm, n), lambda i: (i, 0))
    full = lambda a: pl.BlockSpec(a.shape, lambda i: (0, 0))
    return pl.pallas_call(
        _in_even_kernel,
        grid=(t // tm,),
        in_specs=[row(d), full(g), full(wqkv), full(wz), full(wab), full(wgb), full(wgc),
                  full(whin)],
        out_specs=[row(nqkv), row(nz), row(LANES), row(nsc), row(nsc)],
        out_shape=[jax.ShapeDtypeStruct((t, nqkv), F32),
                   jax.ShapeDtypeStruct((t, nz), BF16),
                   jax.ShapeDtypeStruct((t, LANES), F32),
                   jax.ShapeDtypeStruct((t, nsc), BF16),
                   jax.ShapeDtypeStruct((t, nsc), F32)],
        compiler_params=_params(("parallel",)),
        name="in_even",
    )(x, g, wqkv, wz, wab, wgb, wgc, whin)


def _unit_lower_inverse_many(a_list):
    c = a_list[0].shape[0]
    ii = lax.broadcasted_iota(jnp.int32, (c, c), 0)
    jj = lax.broadcasted_iota(jnp.int32, (c, c), 1)
    eye = (ii == jj).astype(F32)
    same_block = (ii // INV_BLOCK) == (jj // INV_BLOCK)
    a0 = [jnp.where(same_block, a, 0.0) for a in a_list]
    a1 = [a - z for a, z in zip(a_list, a0)]
    d_inv = [eye - z for z in a0]
    power = a0
    for _ in range(INV_BLOCK.bit_length() - 2):
        power = [_mm(p, p) for p in power]
        d_inv = [d + _mm(d, p) for d, p in zip(d_inv, power)]
    n = [_mm(d, r) for d, r in zip(d_inv, a1)]
    m = [eye - x for x in n]
    power = n
    for _ in range(max((c // INV_BLOCK).bit_length() - 2, 0)):
        power = [_mm(p, p) for p in power]
        m = [x + _mm(x, p) for x, p in zip(m, power)]
    return [_mm(x, d) for x, d in zip(m, d_inv)]


def _gdn_kernel(ns, nc, qkv_ref, z_ref, ab_ref, gb_ref, u_ref, conv0_ref, s0_ref, sc0_ref,
                convw_ref, alog_ref, dtb_ref, ng_ref, scw_ref,
                out_ref, s_ref, conv_ref, sc_ref, xbuf, ubuf):
    ch = CHUNK
    top = SUBLANES
    kc, ks = GDN_CONV - 1, SC_CONV - 1
    rows = nc * ch
    wa = GDN_HEADS * GDN_DV
    nk = GDN_HEADS * GDN_DK

    @pl.when(pl.program_id(1) == 0)
    def _():
        conv_ref[...] = conv0_ref[...]
        sc_ref[...] = sc0_ref[...]
        s_ref[...] = s0_ref[...]

    for s in range(ns):
        xbuf[s, top - kc:top, :] = conv_ref[s]
        ubuf[s, top - ks:top, :] = sc_ref[s]
        xbuf[s, top:top + rows, :] = qkv_ref[s]
        ubuf[s, top:top + rows, :] = u_ref[s]
        conv_ref[s] = qkv_ref[s, rows - kc:rows, :]
        sc_ref[s] = u_ref[s, rows - ks:rows, :]

    cw = convw_ref[...]
    sw = scw_ref[...]
    alog, dtb, ng = alog_ref[...], dtb_ref[...], ng_ref[...]
    ii = lax.broadcasted_iota(jnp.int32, (ch, ch), 0)
    jj = lax.broadcasted_iota(jnp.int32, (ch, ch), 1)
    incl = ii >= jj
    strict = ii > jj
    tri = incl.astype(BF16)

    blocks = [(s, k) for s in range(ns) for k in range(nc)]
    chains = [(b, h) for b in range(len(blocks)) for h in range(GDN_HEADS)]

    ys, gcums, gcum_ts, betas = [], [], [], []
    for s, k in blocks:
        base = top - kc + k * ch
        y = xbuf[s, base:base + ch, :] * cw[0:1, :]
        for i in range(1, GDN_CONV):
            y = y + xbuf[s, base + i:base + i + ch, :] * cw[i:i + 1, :]
        ys.append(_silu(y))
        ab = ab_ref[s, k * ch:(k + 1) * ch, :]
        g = -jnp.exp(alog) * _softplus(ab + dtb)
        betas.append(_sigmoid(ab))
        g_hi, g_mid, g_lo = _split3(g)
        gcum = (jnp.dot(tri, g_hi, preferred_element_type=F32)
                + jnp.dot(tri, g_mid, preferred_element_type=F32)
                + jnp.dot(tri, g_lo, preferred_element_type=F32))
        gcums.append(gcum)
        gcum_ts.append(gcum.T)

    def l2n(x):
        return x * lax.rsqrt(jnp.sum(x * x, axis=-1, keepdims=True) + EPS)

    q = [l2n(ys[b][:, h * GDN_DK:(h + 1) * GDN_DK]) * (GDN_DK ** -0.5) for b, h in chains]
    kk = [l2n(ys[b][:, nk + h * GDN_DK:nk + (h + 1) * GDN_DK]) for b, h in chains]
    v = [ys[b][:, 2 * nk + h * GDN_DV:2 * nk + (h + 1) * GDN_DV] for b, h in chains]
    gc = [gcums[b][:, h:h + 1] for b, h in chains]
    gr = [gcum_ts[b][h:h + 1, :] for b, h in chains]
    g_last = [gcums[b][ch - 1:ch, h:h + 1] for b, h in chains]
    bh = [betas[b][:, GDN_HEADS + h:GDN_HEADS + h + 1] for b, h in chains]
    decay = [jnp.exp(jnp.where(incl, c_ - r_, -jnp.inf)) for c_, r_ in zip(gc, gr)]
    kb = [x * b_ for x, b_ in zip(kk, bh)]
    a = [_mm_nt(x, y_) * jnp.where(strict, d, 0.0) for x, y_, d in zip(kb, kk, decay)]
    t_inv = _unit_lower_inverse_many(a)
    eg = [jnp.exp(c_) for c_ in gc]
    rhs = [jnp.concatenate([v_ * b_, x * e], axis=-1) for v_, b_, x, e in zip(v, bh, kb, eg)]
    sol = [_mm(t, r).astype(BF16) for t, r in zip(t_inv, rhs)]
    p = [(_mm_nt(x, y_) * d).astype(BF16) for x, y_, d in zip(q, kk, decay)]
    kd = [(x * jnp.exp(l - c_)).astype(BF16) for x, l, c_ in zip(kk, g_last, gc)]
    ksol = [lax.dot_general(x, y_, (((0,), (0,)), ((), ())), preferred_element_type=F32)
            for x, y_ in zip(kd, sol)]
    psol = [jnp.dot(x, y_, preferred_element_type=F32) for x, y_ in zip(p, sol)]
    lhs = [jnp.concatenate([-ks_[:, GDN_DV:], x * e - ps_[:, GDN_DV:]], axis=0).astype(BF16)
           for ks_, x, e, ps_ in zip(ksol, q, eg, psol)]
    sdecay = [jnp.exp(l) for l in g_last]

    state = [[s_ref[s, h] for h in range(GDN_HEADS)] for s in range(ns)]
    for k in range(nc):
        idx = [(s, h, (s * nc + k) * GDN_HEADS + h) for s in range(ns) for h in range(GDN_HEADS)]
        r = [jnp.dot(lhs[c], state[s][h].astype(BF16), preferred_element_type=F32)
             for s, h, c in idx]
        for (s, h, c), r_ in zip(idx, r):
            state[s][h] = state[s][h] * sdecay[c] + r_[:GDN_DK] + ksol[c][:, :GDN_DV]
        for (s, h, c), r_ in zip(idx, r):
            o = r_[GDN_DK:] + psol[c][:, :GDN_DV]
            zh = z_ref[s, k * ch:(k + 1) * ch, h * GDN_DV:(h + 1) * GDN_DV].astype(F32)
            o = _rms_norm(o, ng) * _silu(zh)
            out_ref[s, k * ch:(k + 1) * ch, h * GDN_DV:(h + 1) * GDN_DV] = o.astype(out_ref.dtype)
    for s in range(ns):
        for h in range(GDN_HEADS):
            s_ref[s, h] = state[s][h]

    for s in range(ns):
        yc = ubuf[s, top - ks:top - ks + rows, :] * sw[0:1, :]
        for i in range(1, SC_CONV):
            yc = yc + ubuf[s, top - ks + i:top - ks + i + rows, :] * sw[i:i + 1, :]
        ob = gb_ref[s].astype(F32) * yc
        out_ref[s, :, wa:] = ob.astype(out_ref.dtype)


def _gdn_sconv(qkv, z, ab, gb, u, conv0, s0, sc0, convw, alog, dtb, ng, scw, nseq):
    t = qkv.shape[0]
    seq = t // nseq
    nchunk = seq // CHUNK
    ns = min(nseq, GDN_BLOCKS_PER_STEP)
    nc = max(1, min(nchunk, GDN_BLOCKS_PER_STEP // ns))
    rows = nc * CHUNK
    nqkv, nz, nsc = qkv.shape[1], z.shape[1], u.shape[1]
    tok = lambda a: a.reshape(nseq, seq, a.shape[1])
    row = lambda n: pl.BlockSpec((ns, rows, n), lambda b, c: (b, c, 0))
    seq3 = lambda a: pl.BlockSpec((ns,) + a.shape[1:], lambda b, c: (b, 0, 0))
    seq4 = lambda a: pl.BlockSpec((ns,) + a.shape[1:], lambda b, c: (b, 0, 0, 0))
    full = lambda a: pl.BlockSpec(a.shape, lambda b, c: (0, 0))
    mix, s_fin, conv_fin, sc_fin = pl.pallas_call(
        functools.partial(_gdn_kernel, ns, nc),
        grid=(nseq // ns, nchunk // nc),
        in_specs=[row(nqkv), row(nz), row(LANES), row(nsc), row(nsc),
                  seq3(conv0), seq4(s0), seq3(sc0),
                  full(convw), full(alog), full(dtb), full(ng), full(scw)],
        out_specs=[row(nz + nsc), seq4(s0), seq3(conv0), seq3(sc0)],
        out_shape=[jax.ShapeDtypeStruct((nseq, seq, nz + nsc), BF16),
                   jax.ShapeDtypeStruct(s0.shape, F32),
                   jax.ShapeDtypeStruct(conv0.shape, F32),
                   jax.ShapeDtypeStruct(sc0.shape, F32)],
        scratch_shapes=[pltpu.VMEM((ns, SUBLANES + rows, nqkv), F32),
                        pltpu.VMEM((ns, SUBLANES + rows, nsc), F32)],
        compiler_params=_params(("parallel", "arbitrary")),
        name="gdn_sconv",
    )(tok(qkv), tok(z), tok(ab), tok(gb), tok(u), conv0, s0, sc0, convw, alog, dtb, ng, scw)
    return mix.reshape(t, nz + nsc), s_fin, conv_fin, sc_fin


def _out_ffn_kernel(x_ref, mix_ref, wout_ref, g_ref, wg_ref, wu_ref, wd_ref, o_ref,
                    xnew, hbuf, acc):
    j = pl.program_id(1)

    @pl.when(j == 0)
    def _():
        xn = x_ref[...] + jnp.dot(mix_ref[...], wout_ref[...], preferred_element_type=F32)
        xnew[...] = xn
        hbuf[...] = _rms_norm(xn, g_ref[...]).astype(BF16)
        acc[...] = jnp.zeros_like(acc)

    h = hbuf[...]
    gate = jnp.dot(h, wg_ref[...], preferred_element_type=F32)
    up = jnp.dot(h, wu_ref[...], preferred_element_type=F32)
    act = (_silu(gate) * up).astype(BF16)
    acc[...] += jnp.dot(act, wd_ref[...], preferred_element_type=F32)

    @pl.when(j == pl.num_programs(1) - 1)
    def _():
        o_ref[...] = xnew[...] + acc[...]


def _ffn_tile(dff):
    best = LANES
    for cand in range(LANES, dff // 2 + 1, LANES):
        if dff % cand == 0:
            best = cand
    return best


def _out_ffn(x, mix, wout, g, wg, wu, wd):
    t, d = x.shape
    dff = wg.shape[1]
    tm = min(TOKEN_TILE, t)
    tf = _ffn_tile(dff)
    return pl.pallas_call(
        _out_ffn_kernel,
        grid=(t // tm, dff // tf),
        in_specs=[pl.BlockSpec((tm, d), lambda i, j: (i, 0)),
                  pl.BlockSpec((tm, mix.shape[1]), lambda i, j: (i, 0)),
                  pl.BlockSpec(wout.shape, lambda i, j: (0, 0)),
                  pl.BlockSpec(g.shape, lambda i, j: (0, 0)),
                  pl.BlockSpec((d, tf), lambda i, j: (0, j)),
                  pl.BlockSpec((d, tf), lambda i, j: (0, j)),
                  pl.BlockSpec((tf, d), lambda i, j: (j, 0))],
        out_specs=pl.BlockSpec((tm, d), lambda i, j: (i, 0)),
        out_shape=jax.ShapeDtypeStruct((t, d), F32),
        scratch_shapes=[pltpu.VMEM((tm, d), F32), pltpu.VMEM((tm, d), BF16),
                        pltpu.VMEM((tm, d), F32)],
        compiler_params=_params(("parallel", "arbitrary")),
        name="out_ffn",
    )(x, mix, wout, g, wg, wu, wd)


def _in_odd_kernel(x_ref, g_ref, wq_ref, wk_ref, wv_ref, wf_ref, bf_ref, qg_ref, kg_ref, *rest):
    q_ref, k_ref, kb_ref, v_ref, vb_ref, lf_ref = rest[-6:]
    tm = x_ref.shape[0]
    h = _rms_norm(x_ref[...], g_ref[...]).astype(BF16)
    q = jnp.dot(h, wq_ref[...], preferred_element_type=F32)
    k = jnp.dot(h, wk_ref[...], preferred_element_type=F32)
    v = jnp.dot(h, wv_ref[...], preferred_element_type=F32)
    qg, kg = qg_ref[...], kg_ref[...]
    qscale = (FOX_DH ** -0.5) * LOG2E
    for hd in range(FOX_HEADS):
        sl = slice(hd * FOX_DH, (hd + 1) * FOX_DH)
        qn = _rms_norm(q[:, sl], qg)
        kn = _rms_norm(k[:, sl], kg)
        q_ref[:, sl] = (qn * qscale).astype(BF16)
        k_ref[0, pl.ds(hd, tm, stride=FOX_HEADS), :] = kn
        kb_ref[:, sl] = kn.astype(BF16)
        v_ref[0, pl.ds(hd, tm, stride=FOX_HEADS), :] = v[:, sl]
    vb_ref[...] = v.astype(BF16)
    f = jnp.dot(h, wf_ref[...], preferred_element_type=F32) + bf_ref[...]
    lf_ref[...] = (-_softplus(-f))[:, :FOX_HEADS]


def _in_odd(x, g, wq, wk, wv, wf, bf, qg, kg, slot, nslots, k_all, v_all):
    t, d = x.shape
    tm = min(TOKEN_TILE, t)
    w = wq.shape[1]
    row = lambda n: pl.BlockSpec((tm, n), lambda i: (i, 0))
    full = lambda a: pl.BlockSpec(a.shape, lambda i: (0, 0))
    stacked = pl.BlockSpec((1, tm * FOX_HEADS, FOX_DH), lambda i: (slot, i, 0))
    stacked_shape = jax.ShapeDtypeStruct((nslots, t * FOX_HEADS, FOX_DH), F32)
    args = [x, g, wq, wk, wv, wf, bf, qg, kg]
    in_specs = [row(d), full(g), full(wq), full(wk), full(wv), full(wf), full(bf),
                full(qg), full(kg)]
    aliases = {}
    if k_all is not None:
        aliases = {len(args): 1, len(args) + 1: 3}
        args += [k_all, v_all]
        in_specs += [pl.BlockSpec(memory_space=pl.ANY)] * 2
    return pl.pallas_call(
        _in_odd_kernel,
        grid=(t // tm,),
        in_specs=in_specs,
        out_specs=[row(w), stacked, row(w), stacked, row(w), row(FOX_HEADS)],
        out_shape=[jax.ShapeDtypeStruct((t, w), BF16), stacked_shape,
                   jax.ShapeDtypeStruct((t, w), BF16), stacked_shape,
                   jax.ShapeDtypeStruct((t, w), BF16),
                   jax.ShapeDtypeStruct((t, FOX_HEADS), F32)],
        input_output_aliases=aliases,
        compiler_params=_params(("parallel",)),
        name="in_odd",
    )(*args)


def _key_bias_kernel(lf_ref, o_ref, carry):
    j = pl.program_id(1)

    @pl.when(j == 0)
    def _():
        carry[...] = jnp.zeros_like(carry)

    lf = lf_ref[0]
    tt = lf.shape[1]
    upper = (lax.broadcasted_iota(jnp.int32, (tt, tt), 0)
             <= lax.broadcasted_iota(jnp.int32, (tt, tt), 1)).astype(BF16)
    hi, mid, lo = _split3(lf)
    csum = (jnp.dot(hi, upper, preferred_element_type=F32)
            + jnp.dot(mid, upper, preferred_element_type=F32)
            + jnp.dot(lo, upper, preferred_element_type=F32)) + carry[:, 0:1]
    o_ref[0] = csum * (-LOG2E)
    carry[...] = jnp.broadcast_to(csum[:, tt - 1:tt], carry.shape)


def _key_bias(lf_t, tile):
    b, h, s = lf_t.shape
    tt = tile if s % tile == 0 else s
    return pl.pallas_call(
        _key_bias_kernel,
        grid=(b, s // tt),
        in_specs=[pl.BlockSpec((1, h, tt), lambda i, j: (i, 0, j))],
        out_specs=pl.BlockSpec((1, h, tt), lambda i, j: (i, 0, j)),
        out_shape=jax.ShapeDtypeStruct((b, h, s), F32),
        scratch_shapes=[pltpu.VMEM((h, LANES), F32)],
        compiler_params=_params(("parallel", "arbitrary")),
        name="key_bias",
    )(lf_t)


def _attn_prompt_kernel(q_ref, k_ref, v_ref, nb_ref, o_ref, m_ref, acc_ref):
    i, j = pl.program_id(1), pl.program_id(2)
    tq, tk = q_ref.shape[0], k_ref.shape[0]
    heads = range(FOX_HEADS)
    sls = [slice(h * FOX_DH, (h + 1) * FOX_DH) for h in heads]

    @pl.when(j == 0)
    def _():
        m_ref[...] = jnp.full_like(m_ref, NEG_BIG)
        acc_ref[...] = jnp.zeros_like(acc_ref)

    def step(masked):
        nb = nb_ref[0]
        ones = jnp.ones((tk, LANES), BF16)
        s = [lax.dot_general(q_ref[:, sl], k_ref[:, sl], (((1,), (1,)), ((), ())),
                             preferred_element_type=F32) + nb[h:h + 1, :]
             for h, sl in zip(heads, sls)]
        if masked:
            causal = (lax.broadcasted_iota(jnp.int32, (tq, tk), 0)
                      >= lax.broadcasted_iota(jnp.int32, (tq, tk), 1))
            s = [jnp.where(causal, x, NEG_BIG) for x in s]
        m_prev = [m_ref[h] for h in heads]
        m_new = [jnp.maximum(mp, jnp.max(x, axis=-1, keepdims=True)) for mp, x in zip(m_prev, s)]
        alpha = [jnp.exp2(mp - mn) for mp, mn in zip(m_prev, m_new)]
        reps = tk // LANES
        p = [jnp.exp2(x - jnp.concatenate([mn] * reps, axis=-1)).astype(BF16)
             for x, mn in zip(s, m_new)]
        pv = [jnp.dot(x, jnp.concatenate([v_ref[:, sl], ones], axis=-1),
                      preferred_element_type=F32) for x, sl in zip(p, sls)]
        for h in heads:
            acc_ref[h] = jnp.concatenate([alpha[h], alpha[h]], axis=-1) * acc_ref[h] + pv[h]
            m_ref[h] = m_new[h]

    @pl.when(j < i)
    def _():
        step(False)

    @pl.when(j == i)
    def _():
        step(True)
        for h, sl in zip(heads, sls):
            acc = acc_ref[h]
            o_ref[:, sl] = (acc[:, :FOX_DH] / acc[:, FOX_DH:]).astype(o_ref.dtype)


def _attn_prompt(q, k, v, nbias, nseq):
    t, w = q.shape
    s = t // nseq
    tile = min(ATTN_TILE, s)
    nblk = s // tile
    kv_spec = pl.BlockSpec((tile, w), lambda b, i, j: (b * nblk + jnp.minimum(j, i), 0))
    return pl.pallas_call(
        _attn_prompt_kernel,
        grid=(nseq, nblk, nblk),
        in_specs=[pl.BlockSpec((tile, w), lambda b, i, j: (b * nblk + i, 0)),
                  kv_spec, kv_spec,
                  pl.BlockSpec((1, FOX_HEADS, tile), lambda b, i, j: (b, 0, jnp.minimum(j, i)))],
        out_specs=pl.BlockSpec((tile, w), lambda b, i, j: (b * nblk + i, 0)),
        out_shape=jax.ShapeDtypeStruct((t, w), BF16),
        scratch_shapes=[pltpu.VMEM((FOX_HEADS, tile, LANES), F32),
                        pltpu.VMEM((FOX_HEADS, tile, FOX_DH + LANES), F32)],
        compiler_params=_params(("parallel", "parallel", "arbitrary")),
        name="attn_prompt",
    )(q, k, v, nbias)


def _attn_sample_kernel(q_ref, kc_ref, vc_ref, kn_ref, vn_ref, nb_ref, o_ref):
    tq = q_ref.shape[0]
    past = kc_ref.shape[2] // FOX_HEADS
    nb = nb_ref[0]
    causal = (lax.broadcasted_iota(jnp.int32, (tq, tq), 0)
              >= lax.broadcasted_iota(jnp.int32, (tq, tq), 1))
    for h in range(FOX_HEADS):
        sl = slice(h * FOX_DH, (h + 1) * FOX_DH)
        qh = q_ref[:, sl]
        kc = kc_ref[0, 0, pl.ds(h, past, stride=FOX_HEADS), :].astype(BF16)
        vc = vc_ref[0, 0, pl.ds(h, past, stride=FOX_HEADS), :].astype(BF16)
        s_c = lax.dot_general(qh, kc, (((1,), (1,)), ((), ())),
                              preferred_element_type=F32) + nb[h:h + 1, :past]
        s_n = lax.dot_general(qh, kn_ref[:, sl], (((1,), (1,)), ((), ())),
                              preferred_element_type=F32) + nb[h:h + 1, past:]
        s_n = jnp.where(causal, s_n, NEG_BIG)
        m = jnp.maximum(jnp.max(s_c, axis=-1, keepdims=True),
                        jnp.max(s_n, axis=-1, keepdims=True))
        p_c = jnp.exp2(s_c - m)
        p_n = jnp.exp2(s_n - m)
        l = jnp.sum(p_c, axis=-1, keepdims=True) + jnp.sum(p_n, axis=-1, keepdims=True)
        o = (jnp.dot(p_c.astype(BF16), vc, preferred_element_type=F32)
             + jnp.dot(p_n.astype(BF16), vn_ref[:, sl], preferred_element_type=F32))
        o_ref[:, sl] = (o / l).astype(o_ref.dtype)


def _attn_sample(q, k_new, v_new, k_cache, v_cache, layer, nbias, nseq):
    t, w = q.shape
    tq = t // nseq
    past = k_cache.shape[2] // FOX_HEADS
    row = pl.BlockSpec((tq, w), lambda b: (b, 0))
    cache = pl.BlockSpec((1, 1) + k_cache.shape[2:], lambda b: (layer, b, 0, 0))
    return pl.pallas_call(
        _attn_sample_kernel,
        grid=(nseq,),
        in_specs=[row, cache, cache, row, row,
                  pl.BlockSpec((1, FOX_HEADS, past + tq), lambda b: (b, 0, 0))],
        out_specs=row,
        out_shape=jax.ShapeDtypeStruct((t, w), BF16),
        compiler_params=_params(("parallel",)),
        name="attn_sample",
    )(q, k_cache, v_cache, k_new, v_new, nbias)


def _pad_lanes(w):
    return jnp.pad(w, ((0, 0), (0, LANES - w.shape[1])))


def _row(v, width=None):
    v = v.reshape(1, -1).astype(F32)
    if width is not None:
        v = jnp.pad(v, ((0, 0), (0, width - v.shape[1])))
    return v


def _prep_weights(norm_mix_g, norm_ffn_g, w_in_even, gdn_conv_w, gdn_a_log, gdn_dt_bias,
                  gdn_norm_g, sconv_w, w_out_even, w_in_odd, fox_b_f, fox_q_norm_g,
                  fox_k_norm_g, w_out_odd, ffn_w_gate, ffn_w_up, ffn_w_down):
    depth = norm_mix_g.shape[0]
    nqkv = gdn_conv_w.shape[2]
    nz = GDN_HEADS * GDN_DV
    nsc = sconv_w.shape[2]
    o1, o2 = nqkv, nqkv + nz
    o4 = o2 + 2 * GDN_HEADS
    o5, o6 = o4 + nsc, o4 + 2 * nsc
    fw = FOX_HEADS * FOX_DH
    layers = []
    for layer in range(depth):
        i = layer // 2
        p = dict(g_mix=_row(norm_mix_g[layer]), g_ffn=_row(norm_ffn_g[layer]),
                 wg=ffn_w_gate[layer].astype(BF16), wu=ffn_w_up[layer].astype(BF16),
                 wd=ffn_w_down[layer].astype(BF16))
        if layer % 2 == 0:
            w = w_in_even[i]
            p.update(wqkv=w[:, :o1].astype(BF16), wz=w[:, o1:o2].astype(BF16),
                     wab=_pad_lanes(w[:, o2:o4]).astype(BF16),
                     wgb=w[:, o4:o5].astype(BF16), wgc=w[:, o5:o6].astype(BF16),
                     whin=w[:, o6:].astype(BF16),
                     convw=gdn_conv_w[i].astype(F32), alog=_row(gdn_a_log[i], LANES),
                     dtb=_row(gdn_dt_bias[i], LANES), ng=_row(gdn_norm_g[i]),
                     scw=sconv_w[i].astype(F32), wout=w_out_even[i].astype(BF16))
        else:
            w = w_in_odd[i]
            p.update(wq=w[:, :fw].astype(BF16), wk=w[:, fw:2 * fw].astype(BF16),
                     wv=w[:, 2 * fw:3 * fw].astype(BF16),
                     wf=_pad_lanes(w[:, 3 * fw:]).astype(BF16),
                     bf=_row(fox_b_f[i], LANES), qg=_row(fox_q_norm_g[i]),
                     kg=_row(fox_k_norm_g[i]), wout=w_out_odd[i].astype(BF16))
        layers.append(p)
    return layers


def _trunk(x, layers, gdn_conv, gdn_s, sconv, fox_k, fox_v, fox_logf):
    nseq, seq, d = x.shape
    x = x.reshape(nseq * seq, d)
    has_past = fox_k is not None
    if has_past:
        fox_k = fox_k.reshape(fox_k.shape[:2] + (-1, FOX_DH))
        fox_v = fox_v.reshape(fox_v.shape[:2] + (-1, FOX_DH))
    new_lf, new_s, new_conv, new_sc = [], [], [], []
    k_all = v_all = None
    for layer, p in enumerate(layers):
        i = layer // 2
        if layer % 2 == 0:
            qkv, z, ab, gb, u = _in_even(x, p["g_mix"], p["wqkv"], p["wz"], p["wab"],
                                         p["wgb"], p["wgc"], p["whin"])
            if has_past:
                conv0, s0, sc0 = gdn_conv[i], gdn_s[i], sconv[i]
            else:
                conv0 = jnp.zeros((nseq, GDN_CONV - 1, qkv.shape[1]), F32)
                s0 = jnp.zeros((nseq, GDN_HEADS, GDN_DK, GDN_DV), F32)
                sc0 = jnp.zeros((nseq, SC_CONV - 1, u.shape[1]), F32)
            mix, s_fin, conv_fin, sc_fin = _gdn_sconv(
                qkv, z, ab, gb, u, conv0, s0, sc0, p["convw"], p["alog"], p["dtb"], p["ng"],
                p["scw"], nseq)
            new_s.append(s_fin)
            new_conv.append(conv_fin)
            new_sc.append(sc_fin)
        else:
            q, k_all, kb, v_all, vb, lf = _in_odd(x, p["g_mix"], p["wq"], p["wk"], p["wv"],
                                                  p["wf"], p["bf"], p["qg"], p["kg"], i,
                                                  len(layers) // 2, k_all, v_all)
            lf = lf.reshape(nseq, seq, FOX_HEADS)
            if has_past:
                lf_all = jnp.concatenate([fox_logf[i].astype(F32), lf], axis=1)
                nbias = _key_bias(lf_all.transpose(0, 2, 1), ATTN_TILE)
                mix = _attn_sample(q, kb, vb, fox_k, fox_v, i, nbias, nseq)
            else:
                nbias = _key_bias(lf.transpose(0, 2, 1), ATTN_TILE)
                mix = _attn_prompt(q, kb, vb, nbias, nseq)
            new_lf.append(lf)
        x = _out_ffn(x, mix, p["wout"], p["g_ffn"], p["wg"], p["wu"], p["wd"])
    kv_shape = (len(layers) // 2, nseq, seq, FOX_HEADS, FOX_DH)
    return (x.reshape(nseq, seq, d), k_all.reshape(kv_shape), v_all.reshape(kv_shape),
            jnp.stack(new_lf), jnp.stack(new_s), jnp.stack(new_conv), jnp.stack(new_sc))


def kernel(x_prompt, x_sample, cache_fox_k, cache_fox_v, cache_fox_logf, state_gdn_S,
           state_gdn_conv, state_sconv, norm_mix_g, norm_ffn_g, w_in_even, gdn_conv_w,
           gdn_a_log, gdn_dt_bias, gdn_norm_g, sconv_w, w_out_even, w_in_odd, fox_b_f,
           fox_q_norm_g, fox_k_norm_g, w_out_odd, ffn_w_gate, ffn_w_up, ffn_w_down):
    layers = _prep_weights(norm_mix_g, norm_ffn_g, w_in_even, gdn_conv_w, gdn_a_log,
                           gdn_dt_bias, gdn_norm_g, sconv_w, w_out_even, w_in_odd, fox_b_f,
                           fox_q_norm_g, fox_k_norm_g, w_out_odd, ffn_w_gate, ffn_w_up,
                           ffn_w_down)
    (y_prompt, p_fox_k, p_fox_v, p_fox_logf, p_gdn_s, p_gdn_conv, p_sconv) = _trunk(
        x_prompt, layers, None, None, None, None, None, None)
    (y_sample, s_fox_k, s_fox_v, s_fox_logf, s_gdn_s, s_gdn_conv, s_sconv) = _trunk(
        x_sample, layers, state_gdn_conv, state_gdn_S, state_sconv, cache_fox_k, cache_fox_v,
        cache_fox_logf)
    return (y_prompt, y_sample, p_fox_k, p_fox_v, p_fox_logf, p_gdn_s, p_gdn_conv, p_sconv,
            s_fox_k, s_fox_v, s_fox_logf, s_gdn_s, s_gdn_conv, s_sconv)
```

```python
import functools

import jax
import jax.numpy as jnp
from jax import lax
from jax.experimental import pallas as pl
from jax.experimental.pallas import tpu as pltpu

F32 = jnp.float32
BF16 = jnp.bfloat16

EPS = 1e-6
CHUNK = 64
GDN_HEADS = 4
GDN_DK = 128
GDN_DV = 128
GDN_CONV = 4
SC_CONV = 3
FOX_HEADS = 8
FOX_DH = 128
LANES = 128
SUBLANES = 8
INV_BLOCK = 16
LOG2E = 1.4426950408889634
NEG_BIG = -1e30
VMEM_LIMIT_BYTES = 56 * 1024 * 1024

TOKEN_TILE = 512
FFN_TOKEN_TILE = 512
MXU_COLS = 256
ATTN_TILE = 512
GDN_BLOCKS_PER_STEP = 8


def _params(semantics):
    return pltpu.CompilerParams(dimension_semantics=semantics,
                                vmem_limit_bytes=VMEM_LIMIT_BYTES)


def _mm(a, b):
    return jnp.dot(a.astype(BF16), b.astype(BF16), preferred_element_type=F32)


def _mm_nt(a, b):
    return lax.dot_general(a.astype(BF16), b.astype(BF16), (((1,), (1,)), ((), ())),
                           preferred_element_type=F32)


def _split3(x):
    hi = x.astype(BF16)
    r1 = x - hi.astype(F32)
    mid = r1.astype(BF16)
    lo = (r1 - mid.astype(F32)).astype(BF16)
    return hi, mid, lo


def _rms_norm(x, g):
    return x * lax.rsqrt(jnp.mean(x * x, axis=-1, keepdims=True) + EPS) * g


def _softplus(x):
    return jnp.maximum(x, 0.0) + jnp.log1p(jnp.exp(-jnp.abs(x)))


def _sigmoid(x):
    return 1.0 / (1.0 + jnp.exp(-x))


def _silu(x):
    return x * _sigmoid(x)


def _in_even_kernel(x_ref, g_ref, wqkv_ref, wz_ref, wab_ref, wgb_ref, wgc_ref, whin_ref,
                    qkv_ref, z_ref, ab_ref, gb_ref, u_ref):
    h = _rms_norm(x_ref[...], g_ref[...]).astype(BF16)
    qkv_ref[...] = jnp.dot(h, wqkv_ref[...], preferred_element_type=F32)
    z_ref[...] = jnp.dot(h, wz_ref[...], preferred_element_type=F32).astype(BF16)
    ab_ref[...] = jnp.dot(h, wab_ref[...], preferred_element_type=F32)
    gb_ref[...] = jnp.dot(h, wgb_ref[...], preferred_element_type=F32).astype(BF16)
    gate_c = jnp.dot(h, wgc_ref[...], preferred_element_type=F32)
    h_in = jnp.dot(h, whin_ref[...], preferred_element_type=F32)
    u_ref[...] = gate_c * h_in


def _in_even(x, g, wqkv, wz, wab, wgb, wgc, whin):
    t, d = x.shape
    tm = min(TOKEN_TILE, t)
    nqkv, nz, nsc = wqkv.shape[1], wz.shape[1], wgb.shape[1]
    row = lambda n: pl.BlockSpec((tm, n), lambda i: (i, 0))
    full = lambda a: pl.BlockSpec(a.shape, lambda i: (0, 0))
    return pl.pallas_call(
        _in_even_kernel,
        grid=(t // tm,),
        in_specs=[row(d), full(g), full(wqkv), full(wz), full(wab), full(wgb), full(wgc),
                  full(whin)],
        out_specs=[row(nqkv), row(nz), row(LANES), row(nsc), row(nsc)],
        out_shape=[jax.ShapeDtypeStruct((t, nqkv), F32),
                   jax.ShapeDtypeStruct((t, nz), BF16),
                   jax.ShapeDtypeStruct((t, LANES), F32),
                   jax.ShapeDtypeStruct((t, nsc), BF16),
                   jax.ShapeDtypeStruct((t, nsc), F32)],
        compiler_params=_params(("parallel",)),
        name="in_even",
    )(x, g, wqkv, wz, wab, wgb, wgc, whin)


def _unit_lower_inverse_many(a_list):
    c = a_list[0].shape[0]
    ii = lax.broadcasted_iota(jnp.int32, (c, c), 0)
    jj = lax.broadcasted_iota(jnp.int32, (c, c), 1)
    eye = (ii == jj).astype(F32)
    same_block = (ii // INV_BLOCK) == (jj // INV_BLOCK)
    a0 = [jnp.where(same_block, a, 0.0) for a in a_list]
    a1 = [a - z for a, z in zip(a_list, a0)]
    d_inv = [eye - z for z in a0]
    power = a0
    for _ in range(INV_BLOCK.bit_length() - 2):
        power = [_mm(p, p) for p in power]
        d_inv = [d + _mm(d, p) for d, p in zip(d_inv, power)]
    n = [_mm(d, r) for d, r in zip(d_inv, a1)]
    m = [eye - x for x in n]
    power = n
    for _ in range(max((c // INV_BLOCK).bit_length() - 2, 0)):
        power = [_mm(p, p) for p in power]
        m = [x + _mm(x, p) for x, p in zip(m, power)]
    return [_mm(x, d) for x, d in zip(m, d_inv)]


def _gdn_kernel(ns, nc, qkv_ref, z_ref, ab_ref, gb_ref, u_ref, conv0_ref, s0_ref, sc0_ref,
                convw_ref, alog_ref, dtb_ref, ng_ref, scw_ref,
                out_ref, s_ref, conv_ref, sc_ref, xbuf, ubuf):
    ch = CHUNK
    top = SUBLANES
    kc, ks = GDN_CONV - 1, SC_CONV - 1
    rows = nc * ch
    wa = GDN_HEADS * GDN_DV
    nk = GDN_HEADS * GDN_DK

    @pl.when(pl.program_id(1) == 0)
    def _():
        conv_ref[...] = conv0_ref[...]
        sc_ref[...] = sc0_ref[...]
        s_ref[...] = s0_ref[...]

    for s in range(ns):
        xbuf[s, top - kc:top, :] = conv_ref[s]
        ubuf[s, top - ks:top, :] = sc_ref[s]
        xbuf[s, top:top + rows, :] = qkv_ref[s]
        ubuf[s, top:top + rows, :] = u_ref[s]
        conv_ref[s] = qkv_ref[s, rows - kc:rows, :]
        sc_ref[s] = u_ref[s, rows - ks:rows, :]

    cw = convw_ref[...]
    sw = scw_ref[...]
    alog, dtb, ng = alog_ref[...], dtb_ref[...], ng_ref[...]
    ii = lax.broadcasted_iota(jnp.int32, (ch, ch), 0)
    jj = lax.broadcasted_iota(jnp.int32, (ch, ch), 1)
    incl = ii >= jj
    strict = ii > jj
    tri = incl.astype(BF16)

    blocks = [(s, k) for s in range(ns) for k in range(nc)]
    chains = [(b, h) for b in range(len(blocks)) for h in range(GDN_HEADS)]

    ys, gcums, gcum_ts, betas = [], [], [], []
    for s, k in blocks:
        base = top - kc + k * ch
        y = xbuf[s, base + kc:base + kc + ch, :] * cw[kc:kc + 1, :]
        for i in range(kc - 1, -1, -1):
            y = y + xbuf[s, base + i:base + i + ch, :] * cw[i:i + 1, :]
        ys.append(_silu(y))
        ab = ab_ref[s, k * ch:(k + 1) * ch, :]
        g = -jnp.exp(alog) * _softplus(ab + dtb)
        betas.append(_sigmoid(ab))
        g_hi, g_mid, g_lo = _split3(g)
        gcum = (jnp.dot(tri, g_hi, preferred_element_type=F32)
                + jnp.dot(tri, g_mid, preferred_element_type=F32)
                + jnp.dot(tri, g_lo, preferred_element_type=F32))
        gcums.append(gcum)
        gcum_ts.append(gcum.T)

    def l2n(x):
        return x * lax.rsqrt(jnp.sum(x * x, axis=-1, keepdims=True) + EPS)

    q = [l2n(ys[b][:, h * GDN_DK:(h + 1) * GDN_DK]) * (GDN_DK ** -0.5) for b, h in chains]
    kk = [l2n(ys[b][:, nk + h * GDN_DK:nk + (h + 1) * GDN_DK]) for b, h in chains]
    v = [ys[b][:, 2 * nk + h * GDN_DV:2 * nk + (h + 1) * GDN_DV] for b, h in chains]
    gc = [gcums[b][:, h:h + 1] for b, h in chains]
    gr = [gcum_ts[b][h:h + 1, :] for b, h in chains]
    g_last = [gcums[b][ch - 1:ch, h:h + 1] for b, h in chains]
    bh = [betas[b][:, GDN_HEADS + h:GDN_HEADS + h + 1] for b, h in chains]
    decay = [jnp.exp(jnp.where(incl, c_ - r_, -jnp.inf)) for c_, r_ in zip(gc, gr)]
    kb = [x * b_ for x, b_ in zip(kk, bh)]
    a = [_mm_nt(x, y_) * jnp.where(strict, d, 0.0) for x, y_, d in zip(kb, kk, decay)]
    t_inv = _unit_lower_inverse_many(a)
    eg = [jnp.exp(c_) for c_ in gc]
    rhs = [jnp.concatenate([v_ * b_, x * e], axis=-1) for v_, b_, x, e in zip(v, bh, kb, eg)]
    sol = [_mm(t, r).astype(BF16) for t, r in zip(t_inv, rhs)]
    p = [(_mm_nt(x, y_) * d).astype(BF16) for x, y_, d in zip(q, kk, decay)]
    kd = [(x * jnp.exp(l - c_)).astype(BF16) for x, l, c_ in zip(kk, g_last, gc)]
    ksol = [lax.dot_general(x, y_, (((0,), (0,)), ((), ())), preferred_element_type=F32)
            for x, y_ in zip(kd, sol)]
    psol = [jnp.dot(x, y_, preferred_element_type=F32) for x, y_ in zip(p, sol)]
    lhs = [jnp.concatenate([-ks_[:, GDN_DV:], x * e - ps_[:, GDN_DV:]], axis=0).astype(BF16)
           for ks_, x, e, ps_ in zip(ksol, q, eg, psol)]
    sdecay = [jnp.exp(l) for l in g_last]

    state = [[s_ref[s, h] for h in range(GDN_HEADS)] for s in range(ns)]
    for k in range(nc):
        idx = [(s, h, (s * nc + k) * GDN_HEADS + h) for s in range(ns) for h in range(GDN_HEADS)]
        r = [jnp.dot(lhs[c], state[s][h].astype(BF16), preferred_element_type=F32)
             for s, h, c in idx]
        for (s, h, c), r_ in zip(idx, r):
            state[s][h] = state[s][h] * sdecay[c] + r_[:GDN_DK] + ksol[c][:, :GDN_DV]
        for (s, h, c), r_ in zip(idx, r):
            o = r_[GDN_DK:] + psol[c][:, :GDN_DV]
            zh = z_ref[s, k * ch:(k + 1) * ch, h * GDN_DV:(h + 1) * GDN_DV].astype(F32)
            o = _rms_norm(o, ng) * _silu(zh)
            out_ref[s, k * ch:(k + 1) * ch, h * GDN_DV:(h + 1) * GDN_DV] = o.astype(out_ref.dtype)
    for s in range(ns):
        for h in range(GDN_HEADS):
            s_ref[s, h] = state[s][h]

    for s in range(ns):
        yc = ubuf[s, top:top + rows, :] * sw[ks:ks + 1, :]
        for i in range(ks - 1, -1, -1):
            yc = yc + ubuf[s, top - ks + i:top - ks + i + rows, :] * sw[i:i + 1, :]
        ob = gb_ref[s].astype(F32) * yc
        out_ref[s, :, wa:] = ob.astype(out_ref.dtype)


def _gdn_sconv(qkv, z, ab, gb, u, conv0, s0, sc0, convw, alog, dtb, ng, scw, nseq):
    t = qkv.shape[0]
    seq = t // nseq
    nchunk = seq // CHUNK
    ns = min(nseq, GDN_BLOCKS_PER_STEP)
    nc = max(1, min(nchunk, GDN_BLOCKS_PER_STEP // ns))
    rows = nc * CHUNK
    nqkv, nz, nsc = qkv.shape[1], z.shape[1], u.shape[1]
    tok = lambda a: a.reshape(nseq, seq, a.shape[1])
    row = lambda n: pl.BlockSpec((ns, rows, n), lambda b, c: (b, c, 0))
    seq3 = lambda a: pl.BlockSpec((ns,) + a.shape[1:], lambda b, c: (b, 0, 0))
    seq4 = lambda a: pl.BlockSpec((ns,) + a.shape[1:], lambda b, c: (b, 0, 0, 0))
    full = lambda a: pl.BlockSpec(a.shape, lambda b, c: (0, 0))
    mix, s_fin, conv_fin, sc_fin = pl.pallas_call(
        functools.partial(_gdn_kernel, ns, nc),
        grid=(nseq // ns, nchunk // nc),
        in_specs=[row(nqkv), row(nz), row(LANES), row(nsc), row(nsc),
                  seq3(conv0), seq4(s0), seq3(sc0),
                  full(convw), full(alog), full(dtb), full(ng), full(scw)],
        out_specs=[row(nz + nsc), seq4(s0), seq3(conv0), seq3(sc0)],
        out_shape=[jax.ShapeDtypeStruct((nseq, seq, nz + nsc), BF16),
                   jax.ShapeDtypeStruct(s0.shape, F32),
                   jax.ShapeDtypeStruct(conv0.shape, F32),
                   jax.ShapeDtypeStruct(sc0.shape, F32)],
        scratch_shapes=[pltpu.VMEM((ns, SUBLANES + rows, nqkv), F32),
                        pltpu.VMEM((ns, SUBLANES + rows, nsc), F32)],
        compiler_params=_params(("parallel", "arbitrary")),
        name="gdn_sconv",
    )(tok(qkv), tok(z), tok(ab), tok(gb), tok(u), conv0, s0, sc0, convw, alog, dtb, ng, scw)
    return mix.reshape(t, nz + nsc), s_fin, conv_fin, sc_fin


def _out_ffn_kernel(x_ref, mix_ref, wout_ref, g_ref, wg_ref, wu_ref, wd_ref, o_ref, act):
    xn = x_ref[...] + jnp.dot(mix_ref[...], wout_ref[...], preferred_element_type=F32)
    h = _rms_norm(xn, g_ref[...]).astype(BF16)
    for c in range(0, wg_ref.shape[1], MXU_COLS):
        gate = jnp.dot(h, wg_ref[:, c:c + MXU_COLS], preferred_element_type=F32)
        up = jnp.dot(h, wu_ref[:, c:c + MXU_COLS], preferred_element_type=F32)
        act[:, c:c + MXU_COLS] = (_silu(gate) * up).astype(BF16)
    o_ref[...] = xn + jnp.dot(act[...], wd_ref[...], preferred_element_type=F32)


def _out_ffn(x, mix, wout, g, wg, wu, wd):
    t, d = x.shape
    dff = wg.shape[1]
    tm = min(FFN_TOKEN_TILE, t)
    row = lambda a: pl.BlockSpec((tm, a.shape[1]), lambda i: (i, 0))
    once = lambda a: pl.BlockSpec(a.shape, lambda i: (0, 0), pipeline_mode=pl.Buffered(1))
    return pl.pallas_call(
        _out_ffn_kernel,
        grid=(t // tm,),
        in_specs=[row(x), row(mix), once(wout), once(g), once(wg), once(wu), once(wd)],
        out_specs=row(x),
        out_shape=jax.ShapeDtypeStruct((t, d), F32),
        scratch_shapes=[pltpu.VMEM((tm, dff), BF16)],
        compiler_params=_params(("parallel",)),
        name="out_ffn",
    )(x, mix, wout, g, wg, wu, wd)


def _in_odd_kernel(x_ref, g_ref, wq_ref, wk_ref, wv_ref, wf_ref, bf_ref, qg_ref, kg_ref, *rest):
    q_ref, k_ref, kb_ref, v_ref, vb_ref, lf_ref = rest[-6:]
    tm = x_ref.shape[0]
    h = _rms_norm(x_ref[...], g_ref[...]).astype(BF16)
    q = jnp.dot(h, wq_ref[...], preferred_element_type=F32)
    k = jnp.dot(h, wk_ref[...], preferred_element_type=F32)
    v = jnp.dot(h, wv_ref[...], preferred_element_type=F32)
    qg, kg = qg_ref[...], kg_ref[...]
    qscale = (FOX_DH ** -0.5) * LOG2E
    for hd in range(FOX_HEADS):
        sl = slice(hd * FOX_DH, (hd + 1) * FOX_DH)
        qn = _rms_norm(q[:, sl], qg)
        kn = _rms_norm(k[:, sl], kg)
        q_ref[:, sl] = (qn * qscale).astype(BF16)
        k_ref[0, pl.ds(hd, tm, stride=FOX_HEADS), :] = kn
        kb_ref[:, sl] = kn.astype(BF16)
        v_ref[0, pl.ds(hd, tm, stride=FOX_HEADS), :] = v[:, sl]
    vb_ref[...] = v.astype(BF16)
    f = jnp.dot(h, wf_ref[...], preferred_element_type=F32) + bf_ref[...]
    lf_ref[...] = (-_softplus(-f))[:, :FOX_HEADS]


def _in_odd(x, g, wq, wk, wv, wf, bf, qg, kg, slot, nslots, k_all, v_all):
    t, d = x.shape
    tm = min(TOKEN_TILE, t)
    w = wq.shape[1]
    row = lambda n: pl.BlockSpec((tm, n), lambda i: (i, 0))
    full = lambda a: pl.BlockSpec(a.shape, lambda i: (0, 0))
    stacked = pl.BlockSpec((1, tm * FOX_HEADS, FOX_DH), lambda i: (slot, i, 0))
    stacked_shape = jax.ShapeDtypeStruct((nslots, t * FOX_HEADS, FOX_DH), F32)
    args = [x, g, wq, wk, wv, wf, bf, qg, kg]
    in_specs = [row(d), full(g), full(wq), full(wk), full(wv), full(wf), full(bf),
                full(qg), full(kg)]
    aliases = {}
    if k_all is not None:
        aliases = {len(args): 1, len(args) + 1: 3}
        args += [k_all, v_all]
        in_specs += [pl.BlockSpec(memory_space=pl.ANY)] * 2
    return pl.pallas_call(
        _in_odd_kernel,
        grid=(t // tm,),
        in_specs=in_specs,
        out_specs=[row(w), stacked, row(w), stacked, row(w), row(FOX_HEADS)],
        out_shape=[jax.ShapeDtypeStruct((t, w), BF16), stacked_shape,
                   jax.ShapeDtypeStruct((t, w), BF16), stacked_shape,
                   jax.ShapeDtypeStruct((t, w), BF16),
                   jax.ShapeDtypeStruct((t, FOX_HEADS), F32)],
        input_output_aliases=aliases,
        compiler_params=_params(("parallel",)),
        name="in_odd",
    )(*args)


def _key_bias_kernel(lf_ref, o_ref, carry):
    @pl.when(pl.program_id(0) == 0)
    def _():
        carry[...] = jnp.zeros_like(carry)

    lf = lf_ref[...]
    tt = lf.shape[1]
    upper = (lax.broadcasted_iota(jnp.int32, (tt, tt), 0)
             <= lax.broadcasted_iota(jnp.int32, (tt, tt), 1)).astype(BF16)
    hi, mid, lo = _split3(lf)
    csum = (jnp.dot(hi, upper, preferred_element_type=F32)
            + jnp.dot(mid, upper, preferred_element_type=F32)
            + jnp.dot(lo, upper, preferred_element_type=F32)) + carry[:, 0:1]
    o_ref[...] = csum * (-LOG2E)
    carry[...] = jnp.broadcast_to(csum[:, tt - 1:tt], carry.shape)


def _key_bias(lf_t, tile):
    b, h, s = lf_t.shape
    tt = tile if s % tile == 0 else s
    return pl.pallas_call(
        _key_bias_kernel,
        grid=(s // tt,),
        in_specs=[pl.BlockSpec((b * h, tt), lambda j: (0, j))],
        out_specs=pl.BlockSpec((b * h, tt), lambda j: (0, j)),
        out_shape=jax.ShapeDtypeStruct((b * h, s), F32),
        scratch_shapes=[pltpu.VMEM((b * h, LANES), F32)],
        compiler_params=_params(("arbitrary",)),
        name="key_bias",
    )(lf_t.reshape(b * h, s)).reshape(b, h, s)


def _attn_prompt_kernel(qi_ref, kj_ref, q_ref, k_ref, v_ref, nb_ref, o_ref, m_ref, acc_ref):
    i, j = qi_ref[pl.program_id(1)], kj_ref[pl.program_id(1)]
    tq, tk = q_ref.shape[0], k_ref.shape[0]
    heads = range(FOX_HEADS)
    sls = [slice(h * FOX_DH, (h + 1) * FOX_DH) for h in heads]

    @pl.when(j == 0)
    def _():
        m_ref[...] = jnp.full_like(m_ref, NEG_BIG)
        acc_ref[...] = jnp.zeros_like(acc_ref)

    def step(masked):
        nb = nb_ref[0]
        ones = jnp.ones((tk, LANES), BF16)
        reps = tk // LANES
        if masked:
            causal = (lax.broadcasted_iota(jnp.int32, (tq, tk), 0)
                      >= lax.broadcasted_iota(jnp.int32, (tq, tk), 1))
        for h, sl in zip(heads, sls):
            s = lax.dot_general(q_ref[:, sl], k_ref[:, sl], (((1,), (1,)), ((), ())),
                                preferred_element_type=F32) + nb[h:h + 1, :]
            if masked:
                s = jnp.where(causal, s, NEG_BIG)
            m_prev = m_ref[h]
            m_new = jnp.maximum(m_prev, jnp.max(s, axis=-1, keepdims=True))
            alpha = jnp.exp2(m_prev - m_new)
            p = jnp.exp2(s - jnp.concatenate([m_new] * reps, axis=-1)).astype(BF16)
            pv = jnp.dot(p, jnp.concatenate([v_ref[:, sl], ones], axis=-1),
                         preferred_element_type=F32)
            acc_ref[h] = jnp.concatenate([alpha, alpha], axis=-1) * acc_ref[h] + pv
            m_ref[h] = m_new

    @pl.when(j < i)
    def _():
        step(False)

    @pl.when(j == i)
    def _():
        step(True)
        for h, sl in zip(heads, sls):
            acc = acc_ref[h]
            o_ref[:, sl] = (acc[:, :FOX_DH] / acc[:, FOX_DH:]).astype(o_ref.dtype)


def _attn_prompt(q, k, v, nbias, nseq):
    t, w = q.shape
    s = t // nseq
    tile = min(ATTN_TILE, s)
    nblk = s // tile
    pairs = [(i, j) for i in range(nblk) for j in range(i + 1)]
    qi = jnp.asarray([p[0] for p in pairs], jnp.int32)
    kj = jnp.asarray([p[1] for p in pairs], jnp.int32)
    q_spec = pl.BlockSpec((tile, w), lambda b, n, qi, kj: (b * nblk + qi[n], 0))
    kv_spec = pl.BlockSpec((tile, w), lambda b, n, qi, kj: (b * nblk + kj[n], 0))
    return pl.pallas_call(
        _attn_prompt_kernel,
        grid_spec=pltpu.PrefetchScalarGridSpec(
            num_scalar_prefetch=2,
            grid=(nseq, len(pairs)),
            in_specs=[q_spec, kv_spec, kv_spec,
                      pl.BlockSpec((1, FOX_HEADS, tile), lambda b, n, qi, kj: (b, 0, kj[n]))],
            out_specs=q_spec,
            scratch_shapes=[pltpu.VMEM((FOX_HEADS, tile, LANES), F32),
                            pltpu.VMEM((FOX_HEADS, tile, FOX_DH + LANES), F32)]),
        out_shape=jax.ShapeDtypeStruct((t, w), BF16),
        compiler_params=_params(("parallel", "arbitrary")),
        name="attn_prompt",
    )(qi, kj, q, k, v, nbias)


def _attn_sample_kernel(q_ref, kc_ref, vc_ref, kn_ref, vn_ref, nb_ref, o_ref):
    tq = q_ref.shape[0]
    past = kc_ref.shape[2] // FOX_HEADS
    nb = nb_ref[0]
    heads = range(FOX_HEADS)
    sls = [slice(h * FOX_DH, (h + 1) * FOX_DH) for h in heads]
    nt = (((1,), (1,)), ((), ()))
    causal = (lax.broadcasted_iota(jnp.int32, (tq, tq), 0)
              >= lax.broadcasted_iota(jnp.int32, (tq, tq), 1))
    kc = [kc_ref[0, 0, pl.ds(h, past, stride=FOX_HEADS), :].astype(BF16) for h in heads]
    s_c = [lax.dot_general(q_ref[:, sl], x, nt, preferred_element_type=F32) + nb[h:h + 1, :past]
           for h, sl, x in zip(heads, sls, kc)]
    s_n = [jnp.where(causal,
                     lax.dot_general(q_ref[:, sl], kn_ref[:, sl], nt,
                                     preferred_element_type=F32) + nb[h:h + 1, past:], NEG_BIG)
           for h, sl in zip(heads, sls)]
    m = [jnp.maximum(jnp.max(a, axis=-1, keepdims=True), jnp.max(b, axis=-1, keepdims=True))
         for a, b in zip(s_c, s_n)]
    p_c = [jnp.exp2(a - mm) for a, mm in zip(s_c, m)]
    p_n = [jnp.exp2(b - mm) for b, mm in zip(s_n, m)]
    l = [jnp.sum(a, axis=-1, keepdims=True) + jnp.sum(b, axis=-1, keepdims=True)
         for a, b in zip(p_c, p_n)]
    vc = [vc_ref[0, 0, pl.ds(h, past, stride=FOX_HEADS), :].astype(BF16) for h in heads]
    o = [jnp.dot(a.astype(BF16), x, preferred_element_type=F32)
         + jnp.dot(b.astype(BF16), vn_ref[:, sl], preferred_element_type=F32)
         for a, b, x, sl in zip(p_c, p_n, vc, sls)]
    for h, sl in zip(heads, sls):
        o_ref[:, sl] = (o[h] / l[h]).astype(o_ref.dtype)


def _attn_sample(q, k_new, v_new, k_cache, v_cache, layer, nbias, nseq):
    t, w = q.shape
    tq = t // nseq
    past = k_cache.shape[2] // FOX_HEADS
    row = pl.BlockSpec((tq, w), lambda b: (b, 0))
    cache = pl.BlockSpec((1, 1) + k_cache.shape[2:], lambda b: (layer, b, 0, 0))
    return pl.pallas_call(
        _attn_sample_kernel,
        grid=(nseq,),
        in_specs=[row, cache, cache, row, row,
                  pl.BlockSpec((1, FOX_HEADS, past + tq), lambda b: (b, 0, 0))],
        out_specs=row,
        out_shape=jax.ShapeDtypeStruct((t, w), BF16),
        compiler_params=_params(("parallel",)),
        name="attn_sample",
    )(q, k_cache, v_cache, k_new, v_new, nbias)


def _pad_lanes(w):
    return jnp.pad(w, ((0, 0), (0, LANES - w.shape[1])))


def _row(v, width=None):
    v = v.reshape(1, -1).astype(F32)
    if width is not None:
        v = jnp.pad(v, ((0, 0), (0, width - v.shape[1])))
    return v


def _prep_weights(norm_mix_g, norm_ffn_g, w_in_even, gdn_conv_w, gdn_a_log, gdn_dt_bias,
                  gdn_norm_g, sconv_w, w_out_even, w_in_odd, fox_b_f, fox_q_norm_g,
                  fox_k_norm_g, w_out_odd, ffn_w_gate, ffn_w_up, ffn_w_down):
    depth = norm_mix_g.shape[0]
    nqkv = gdn_conv_w.shape[2]
    nz = GDN_HEADS * GDN_DV
    nsc = sconv_w.shape[2]
    o1, o2 = nqkv, nqkv + nz
    o4 = o2 + 2 * GDN_HEADS
    o5, o6 = o4 + nsc, o4 + 2 * nsc
    fw = FOX_HEADS * FOX_DH
    layers = []
    for layer in range(depth):
        i = layer // 2
        p = dict(g_mix=_row(norm_mix_g[layer]), g_ffn=_row(norm_ffn_g[layer]),
                 wg=ffn_w_gate[layer].astype(BF16), wu=ffn_w_up[layer].astype(BF16),
                 wd=ffn_w_down[layer].astype(BF16))
        if layer % 2 == 0:
            w = w_in_even[i]
            p.update(wqkv=w[:, :o1].astype(BF16), wz=w[:, o1:o2].astype(BF16),
                     wab=_pad_lanes(w[:, o2:o4]).astype(BF16),
                     wgb=w[:, o4:o5].astype(BF16), wgc=w[:, o5:o6].astype(BF16),
                     whin=w[:, o6:].astype(BF16),
                     convw=gdn_conv_w[i].astype(F32), alog=_row(gdn_a_log[i], LANES),
                     dtb=_row(gdn_dt_bias[i], LANES), ng=_row(gdn_norm_g[i]),
                     scw=sconv_w[i].astype(F32), wout=w_out_even[i].astype(BF16))
        else:
            w = w_in_odd[i]
            p.update(wq=w[:, :fw].astype(BF16), wk=w[:, fw:2 * fw].astype(BF16),
                     wv=w[:, 2 * fw:3 * fw].astype(BF16),
                     wf=_pad_lanes(w[:, 3 * fw:]).astype(BF16),
                     bf=_row(fox_b_f[i], LANES), qg=_row(fox_q_norm_g[i]),
                     kg=_row(fox_k_norm_g[i]), wout=w_out_odd[i].astype(BF16))
        layers.append(p)
    return layers


def _trunk(x, layers, gdn_conv, gdn_s, sconv, fox_k, fox_v, fox_logf):
    nseq, seq, d = x.shape
    x = x.reshape(nseq * seq, d)
    has_past = fox_k is not None
    if has_past:
        fox_k = fox_k.reshape(fox_k.shape[:2] + (-1, FOX_DH))
        fox_v = fox_v.reshape(fox_v.shape[:2] + (-1, FOX_DH))
    new_lf, new_s, new_conv, new_sc = [], [], [], []
    k_all = v_all = None
    for layer, p in enumerate(layers):
        i = layer // 2
        if layer % 2 == 0:
            qkv, z, ab, gb, u = _in_even(x, p["g_mix"], p["wqkv"], p["wz"], p["wab"],
                                         p["wgb"], p["wgc"], p["whin"])
            if has_past:
                conv0, s0, sc0 = gdn_conv[i], gdn_s[i], sconv[i]
            else:
                conv0 = jnp.zeros((nseq, GDN_CONV - 1, qkv.shape[1]), F32)
                s0 = jnp.zeros((nseq, GDN_HEADS, GDN_DK, GDN_DV), F32)
                sc0 = jnp.zeros((nseq, SC_CONV - 1, u.shape[1]), F32)
            mix, s_fin, conv_fin, sc_fin = _gdn_sconv(
                qkv, z, ab, gb, u, conv0, s0, sc0, p["convw"], p["alog"], p["dtb"], p["ng"],
                p["scw"], nseq)
            new_s.append(s_fin)
            new_conv.append(conv_fin)
            new_sc.append(sc_fin)
        else:
            q, k_all, kb, v_all, vb, lf = _in_odd(x, p["g_mix"], p["wq"], p["wk"], p["wv"],
                                                  p["wf"], p["bf"], p["qg"], p["kg"], i,
                                                  len(layers) // 2, k_all, v_all)
            lf = lf.reshape(nseq, seq, FOX_HEADS)
            if has_past:
                lf_all = jnp.concatenate([fox_logf[i].astype(F32), lf], axis=1)
                nbias = _key_bias(lf_all.transpose(0, 2, 1), ATTN_TILE)
                mix = _attn_sample(q, kb, vb, fox_k, fox_v, i, nbias, nseq)
            else:
                nbias = _key_bias(lf.transpose(0, 2, 1), ATTN_TILE)
                mix = _attn_prompt(q, kb, vb, nbias, nseq)
            new_lf.append(lf)
        x = _out_ffn(x, mix, p["wout"], p["g_ffn"], p["wg"], p["wu"], p["wd"])
    kv_shape = (len(layers) // 2, nseq, seq, FOX_HEADS, FOX_DH)
    return (x.reshape(nseq, seq, d), k_all.reshape(kv_shape), v_all.reshape(kv_shape),
            jnp.stack(new_lf), jnp.stack(new_s), jnp.stack(new_conv), jnp.stack(new_sc))


def kernel(x_prompt, x_sample, cache_fox_k, cache_fox_v, cache_fox_logf, state_gdn_S,
           state_gdn_conv, state_sconv, norm_mix_g, norm_ffn_g, w_in_even, gdn_conv_w,
           gdn_a_log, gdn_dt_bias, gdn_norm_g, sconv_w, w_out_even, w_in_odd, fox_b_f,
           fox_q_norm_g, fox_k_norm_g, w_out_odd, ffn_w_gate, ffn_w_up, ffn_w_down):
    layers = _prep_weights(norm_mix_g, norm_ffn_g, w_in_even, gdn_conv_w, gdn_a_log,
                           gdn_dt_bias, gdn_norm_g, sconv_w, w_out_even, w_in_odd, fox_b_f,
                           fox_q_norm_g, fox_k_norm_g, w_out_odd, ffn_w_gate, ffn_w_up,
                           ffn_w_down)
    (y_prompt, p_fox_k, p_fox_v, p_fox_logf, p_gdn_s, p_gdn_conv, p_sconv) = _trunk(
        x_prompt, layers, None, None, None, None, None, None)
    (y_sample, s_fox_k, s_fox_v, s_fox_logf, s_gdn_s, s_gdn_conv, s_sconv) = _trunk(
        x_sample, layers, state_gdn_conv, state_gdn_S, state_sconv, cache_fox_k, cache_fox_v,
        cache_fox_logf)
    return (y_prompt, y_sample, p_fox_k, p_fox_v, p_fox_logf, p_gdn_s, p_gdn_conv, p_sconv,
            s_fox_k, s_fox_v, s_fox_logf, s_gdn_s, s_gdn_conv, s_sconv)
```

```python
import functools

import jax
import jax.numpy as jnp
from jax import lax
from jax.experimental import pallas as pl
from jax.experimental.pallas import tpu as pltpu

F32 = jnp.float32
BF16 = jnp.bfloat16

EPS = 1e-6
CHUNK = 64
GDN_HEADS = 4
GDN_DK = 128
GDN_DV = 128
GDN_CONV = 4
SC_CONV = 3
FOX_HEADS = 8
FOX_DH = 128
LANES = 128
SUBLANES = 8
INV_BLOCK = 16
LOG2E = 1.4426950408889634
NEG_BIG = -1e30
VMEM_LIMIT_BYTES = 56 * 1024 * 1024

TOKEN_TILE = 512
FFN_TOKEN_TILE = 512
MXU_COLS = 256
ATTN_TILE = 512
GDN_BLOCKS_PER_STEP = 8


def _params(semantics):
    return pltpu.CompilerParams(dimension_semantics=semantics,
                                vmem_limit_bytes=VMEM_LIMIT_BYTES)


def _mm(a, b):
    return jnp.dot(a.astype(BF16), b.astype(BF16), preferred_element_type=F32)


def _mm_nt(a, b):
    return lax.dot_general(a.astype(BF16), b.astype(BF16), (((1,), (1,)), ((), ())),
                           preferred_element_type=F32)


def _split3(x):
    hi = x.astype(BF16)
    r1 = x - hi.astype(F32)
    mid = r1.astype(BF16)
    lo = (r1 - mid.astype(F32)).astype(BF16)
    return hi, mid, lo


def _rms_norm(x, g):
    return x * lax.rsqrt(jnp.mean(x * x, axis=-1, keepdims=True) + EPS) * g


def _softplus(x):
    return jnp.maximum(x, 0.0) + jnp.log1p(jnp.exp(-jnp.abs(x)))


def _sigmoid(x):
    return 1.0 / (1.0 + jnp.exp(-x))


def _silu(x):
    return x * _sigmoid(x)


def _in_even_kernel(ns, rows, x_ref, g_ref, wqkv_ref, wz_ref, wab_ref, wgb_ref, wgc_ref,
                    whin_ref, conv0_ref, sc0_ref, convw_ref, scw_ref,
                    y_ref, z_ref, ab_ref, ob_ref, conv_ref, sc_ref, xslab, uslab):
    top = SUBLANES
    kc, ks = GDN_CONV - 1, SC_CONV - 1

    @pl.when(pl.program_id(1) == 0)
    def _():
        conv_ref[...] = conv0_ref[...]
        sc_ref[...] = sc0_ref[...]

    x = x_ref[...].reshape(ns * rows, x_ref.shape[2])
    h = _rms_norm(x, g_ref[...]).astype(BF16)
    cw = convw_ref[...]
    sw = scw_ref[...]

    def causal_conv(t, c0, slab, tail_ref, w, taps, s):
        for j in range(t.shape[1] // LANES):
            cs = slice(c0 + j * LANES, c0 + (j + 1) * LANES)
            k = cs.start // LANES
            ts = t[s * rows:(s + 1) * rows, j * LANES:(j + 1) * LANES]
            slab[s, k, top - taps:top, :] = tail_ref[s, :, cs]
            slab[s, k, top:top + rows, :] = ts
            tail_ref[s, :, cs] = ts[rows - taps:rows]
            acc = ts * w[taps:taps + 1, cs]
            for d in range(1, taps + 1):
                acc = acc + slab[s, k, top - d:top - d + rows, :] * w[taps - d:taps - d + 1, cs]
            yield cs, j, acc

    def dot(w_ref, cols):
        return jnp.dot(h, w_ref[:, cols], preferred_element_type=F32)

    jobs = []
    for c in range(0, wqkv_ref.shape[1], MXU_COLS):
        def conv_silu(t, c=c):
            for s in range(ns):
                for cs, _, acc in causal_conv(t, c, xslab, conv_ref, cw, kc, s):
                    y_ref[s, :, cs] = _silu(acc)
        jobs.append((lambda c=c: dot(wqkv_ref, slice(c, c + MXU_COLS)), conv_silu))
    for c in range(0, wgb_ref.shape[1], MXU_COLS):
        def gated_conv(t, c=c):
            gate_c, h_in, gate_b = t
            u = gate_c * h_in
            for s in range(ns):
                for cs, j, acc in causal_conv(u, c, uslab, sc_ref, sw, ks, s):
                    gb = gate_b[s * rows:(s + 1) * rows, j * LANES:(j + 1) * LANES]
                    ob_ref[s, :, cs] = (gb * acc).astype(ob_ref.dtype)
        cols = slice(c, c + MXU_COLS)
        jobs.append((lambda cols=cols: (dot(wgc_ref, cols), dot(whin_ref, cols), dot(wgb_ref, cols)),
                     gated_conv))

    def store_z(t):
        z_ref[...] = t.astype(BF16).reshape(z_ref.shape)

    def store_ab(t):
        ab_ref[...] = t.reshape(ab_ref.shape)
    jobs.append((lambda: jnp.dot(h, wz_ref[...], preferred_element_type=F32), store_z))
    jobs.append((lambda: jnp.dot(h, wab_ref[...], preferred_element_type=F32), store_ab))

    nxt = jobs[0][0]()
    for n, (_, post) in enumerate(jobs):
        cur = nxt
        if n + 1 < len(jobs):
            nxt = jobs[n + 1][0]()
        post(cur)


def _stream_blocking(nseq, seq, rows_per_step):
    rows = min(seq, rows_per_step)
    ns = max(1, min(nseq, rows_per_step // rows))
    return ns, rows


def _in_even(x, g, wqkv, wz, wab, wgb, wgc, whin, conv0, sc0, convw, scw, nseq):
    t, d = x.shape
    seq = t // nseq
    ns, rows = _stream_blocking(nseq, seq, TOKEN_TILE)
    nqkv, nz, nsc = wqkv.shape[1], wz.shape[1], wgb.shape[1]
    row = lambda n: pl.BlockSpec((ns, rows, n), lambda b, c: (b, c, 0))
    seq3 = lambda a: pl.BlockSpec((ns,) + a.shape[1:], lambda b, c: (b, 0, 0))
    full = lambda a: pl.BlockSpec(a.shape, lambda b, c: (0, 0))
    y, z, ab, ob, conv_fin, sc_fin = pl.pallas_call(
        functools.partial(_in_even_kernel, ns, rows),
        grid=(nseq // ns, seq // rows),
        in_specs=[row(d), full(g), full(wqkv), full(wz), full(wab), full(wgb), full(wgc),
                  full(whin), seq3(conv0), seq3(sc0), full(convw), full(scw)],
        out_specs=[row(nqkv), row(nz), row(LANES), row(nsc), seq3(conv0), seq3(sc0)],
        out_shape=[jax.ShapeDtypeStruct((nseq, seq, nqkv), F32),
                   jax.ShapeDtypeStruct((nseq, seq, nz), BF16),
                   jax.ShapeDtypeStruct((nseq, seq, LANES), F32),
                   jax.ShapeDtypeStruct((nseq, seq, nsc), BF16),
                   jax.ShapeDtypeStruct(conv0.shape, F32),
                   jax.ShapeDtypeStruct(sc0.shape, F32)],
        scratch_shapes=[pltpu.VMEM((ns, nqkv // LANES, SUBLANES + rows, LANES), F32),
                        pltpu.VMEM((ns, nsc // LANES, SUBLANES + rows, LANES), F32)],
        compiler_params=_params(("parallel", "arbitrary")),
        name="in_even",
    )(x.reshape(nseq, seq, d), g, wqkv, wz, wab, wgb, wgc, whin, conv0, sc0, convw, scw)
    return y, z, ab, ob.reshape(t, nsc), conv_fin, sc_fin


def _unit_lower_inverse_many(a_list):
    c = a_list[0].shape[0]
    ii = lax.broadcasted_iota(jnp.int32, (c, c), 0)
    jj = lax.broadcasted_iota(jnp.int32, (c, c), 1)
    eye = (ii == jj).astype(F32)
    same_block = (ii // INV_BLOCK) == (jj // INV_BLOCK)
    a0 = [jnp.where(same_block, a, 0.0) for a in a_list]
    a1 = [a - z for a, z in zip(a_list, a0)]
    d_inv = [eye - z for z in a0]
    power = a0
    for _ in range(INV_BLOCK.bit_length() - 2):
        power = [_mm(p, p) for p in power]
        d_inv = [d + _mm(d, p) for d, p in zip(d_inv, power)]
    n = [_mm(d, r) for d, r in zip(d_inv, a1)]
    m = [eye - x for x in n]
    power = n
    for _ in range(max((c // INV_BLOCK).bit_length() - 2, 0)):
        power = [_mm(p, p) for p in power]
        m = [x + _mm(x, p) for x, p in zip(m, power)]
    return [_mm(x, d) for x, d in zip(m, d_inv)]


def _gdn_kernel(ns, nc, y_ref, z_ref, ab_ref, s0_ref, alog_ref, dtb_ref, ng_ref,
                out_ref, s_ref):
    ch = CHUNK
    nk = GDN_HEADS * GDN_DK

    @pl.when(pl.program_id(1) == 0)
    def _():
        s_ref[...] = s0_ref[...]

    alog, dtb, ng = alog_ref[...], dtb_ref[...], ng_ref[...]
    ii = lax.broadcasted_iota(jnp.int32, (ch, ch), 0)
    jj = lax.broadcasted_iota(jnp.int32, (ch, ch), 1)
    incl = ii >= jj
    strict = ii > jj
    tri = incl.astype(BF16)

    blocks = [(s, k) for s in range(ns) for k in range(nc)]
    chains = [(b, h) for b in range(len(blocks)) for h in range(GDN_HEADS)]

    ys, gcums, gcum_ts, betas = [], [], [], []
    for s, k in blocks:
        ys.append(y_ref[s, k * ch:(k + 1) * ch, :])
        ab = ab_ref[s, k * ch:(k + 1) * ch, :]
        g = -jnp.exp(alog) * _softplus(ab + dtb)
        betas.append(_sigmoid(ab))
        g_hi, g_mid, g_lo = _split3(g)
        gcum = (jnp.dot(tri, g_hi, preferred_element_type=F32)
                + jnp.dot(tri, g_mid, preferred_element_type=F32)
                + jnp.dot(tri, g_lo, preferred_element_type=F32))
        gcums.append(gcum)
        gcum_ts.append(gcum.T)

    def l2n(x):
        return x * lax.rsqrt(jnp.sum(x * x, axis=-1, keepdims=True) + EPS)

    q = [l2n(ys[b][:, h * GDN_DK:(h + 1) * GDN_DK]) * (GDN_DK ** -0.5) for b, h in chains]
    kk = [l2n(ys[b][:, nk + h * GDN_DK:nk + (h + 1) * GDN_DK]) for b, h in chains]
    v = [ys[b][:, 2 * nk + h * GDN_DV:2 * nk + (h + 1) * GDN_DV] for b, h in chains]
    gc = [gcums[b][:, h:h + 1] for b, h in chains]
    gr = [gcum_ts[b][h:h + 1, :] for b, h in chains]
    g_last = [gcums[b][ch - 1:ch, h:h + 1] for b, h in chains]
    bh = [betas[b][:, GDN_HEADS + h:GDN_HEADS + h + 1] for b, h in chains]
    decay = [jnp.exp(jnp.where(incl, c_ - r_, -jnp.inf)) for c_, r_ in zip(gc, gr)]
    kb = [x * b_ for x, b_ in zip(kk, bh)]
    a = [_mm_nt(x, y_) * jnp.where(strict, d, 0.0) for x, y_, d in zip(kb, kk, decay)]
    t_inv = _unit_lower_inverse_many(a)
    eg = [jnp.exp(c_) for c_ in gc]
    rhs = [jnp.concatenate([v_ * b_, x * e], axis=-1) for v_, b_, x, e in zip(v, bh, kb, eg)]
    sol = [_mm(t, r).astype(BF16) for t, r in zip(t_inv, rhs)]
    p = [(_mm_nt(x, y_) * d).astype(BF16) for x, y_, d in zip(q, kk, decay)]
    kd = [(x * jnp.exp(l - c_)).astype(BF16) for x, l, c_ in zip(kk, g_last, gc)]
    ksol = [lax.dot_general(x, y_, (((0,), (0,)), ((), ())), preferred_element_type=F32)
            for x, y_ in zip(kd, sol)]
    psol = [jnp.dot(x, y_, preferred_element_type=F32) for x, y_ in zip(p, sol)]
    lhs = [jnp.concatenate([-ks_[:, GDN_DV:], x * e - ps_[:, GDN_DV:]], axis=0).astype(BF16)
           for ks_, x, e, ps_ in zip(ksol, q, eg, psol)]
    sdecay = [jnp.exp(l) for l in g_last]

    state = [[s_ref[s, h] for h in range(GDN_HEADS)] for s in range(ns)]
    for k in range(nc):
        idx = [(s, h, (s * nc + k) * GDN_HEADS + h) for s in range(ns) for h in range(GDN_HEADS)]
        r = [jnp.dot(lhs[c], state[s][h].astype(BF16), preferred_element_type=F32)
             for s, h, c in idx]
        for (s, h, c), r_ in zip(idx, r):
            state[s][h] = state[s][h] * sdecay[c] + r_[:GDN_DK] + ksol[c][:, :GDN_DV]
        for (s, h, c), r_ in zip(idx, r):
            o = r_[GDN_DK:] + psol[c][:, :GDN_DV]
            zh = z_ref[s, k * ch:(k + 1) * ch, h * GDN_DV:(h + 1) * GDN_DV].astype(F32)
            o = _rms_norm(o, ng) * _silu(zh)
            out_ref[s, k * ch:(k + 1) * ch, h * GDN_DV:(h + 1) * GDN_DV] = o.astype(out_ref.dtype)
    for s in range(ns):
        for h in range(GDN_HEADS):
            s_ref[s, h] = state[s][h]


def _gdn(y, z, ab, s0, alog, dtb, ng):
    nseq, seq, nqkv = y.shape
    nz = z.shape[2]
    nchunk = seq // CHUNK
    ns = min(nseq, GDN_BLOCKS_PER_STEP)
    nc = max(1, min(nchunk, GDN_BLOCKS_PER_STEP // ns))
    rows = nc * CHUNK
    row = lambda n: pl.BlockSpec((ns, rows, n), lambda b, c: (b, c, 0))
    seq4 = lambda a: pl.BlockSpec((ns,) + a.shape[1:], lambda b, c: (b, 0, 0, 0))
    full = lambda a: pl.BlockSpec(a.shape, lambda b, c: (0, 0))
    o, s_fin = pl.pallas_call(
        functools.partial(_gdn_kernel, ns, nc),
        grid=(nseq // ns, nchunk // nc),
        in_specs=[row(nqkv), row(nz), row(LANES), seq4(s0), full(alog), full(dtb), full(ng)],
        out_specs=[row(nz), seq4(s0)],
        out_shape=[jax.ShapeDtypeStruct((nseq, seq, nz), BF16),
                   jax.ShapeDtypeStruct(s0.shape, F32)],
        compiler_params=_params(("parallel", "arbitrary")),
        name="gdn",
    )(y, z, ab, s0, alog, dtb, ng)
    return o.reshape(nseq * seq, nz), s_fin


def _out_ffn_kernel(x_ref, ma_ref, mb_ref, wa_ref, wb_ref, g_ref, wg_ref, wu_ref, wd_ref, o_ref,
                    act):
    xn = (x_ref[...] + jnp.dot(ma_ref[...], wa_ref[0], preferred_element_type=F32)
          + jnp.dot(mb_ref[...], wb_ref[0], preferred_element_type=F32))
    h = _rms_norm(xn, g_ref[0]).astype(BF16)
    for c in range(0, wg_ref.shape[2], MXU_COLS):
        gate = jnp.dot(h, wg_ref[0, :, c:c + MXU_COLS], preferred_element_type=F32)
        up = jnp.dot(h, wu_ref[0, :, c:c + MXU_COLS], preferred_element_type=F32)
        act[:, c:c + MXU_COLS] = (_silu(gate) * up).astype(BF16)
    o_ref[...] = xn + jnp.dot(act[...], wd_ref[0], preferred_element_type=F32)


def _out_ffn(x, mix_a, mix_b, wout, iw, g, wg, wu, wd, layer):
    t, d = x.shape
    tm = min(FFN_TOKEN_TILE, t)
    dff = wg.shape[2]
    half = wout.shape[1] // 2
    row = lambda a: pl.BlockSpec((tm, a.shape[1]), lambda i: (i, 0))
    once = lambda shape, imap: pl.BlockSpec(shape, imap, pipeline_mode=pl.Buffered(1))
    if mix_b is None:
        mix_b = mix_a
        ma_spec = pl.BlockSpec((tm, half), lambda i: (i, 0))
        mb_spec = pl.BlockSpec((tm, half), lambda i: (i, 1))
    else:
        ma_spec, mb_spec = row(mix_a), row(mix_b)
    return pl.pallas_call(
        _out_ffn_kernel,
        grid=(t // tm,),
        in_specs=[row(x), ma_spec, mb_spec,
                  once((1, half, d), lambda i: (iw, 0, 0)),
                  once((1, half, d), lambda i: (iw, 1, 0)),
                  once((1, 1, d), lambda i: (layer, 0, 0)),
                  once((1, d, dff), lambda i: (layer, 0, 0)),
                  once((1, d, dff), lambda i: (layer, 0, 0)),
                  once((1, dff, d), lambda i: (layer, 0, 0))],
        out_specs=row(x),
        out_shape=jax.ShapeDtypeStruct((t, d), F32),
        scratch_shapes=[pltpu.VMEM((tm, dff), BF16)],
        compiler_params=_params(("parallel",)),
        name="out_ffn",
    )(x, mix_a, mix_b, wout, wout, g, wg, wu, wd)


def _in_odd_kernel(x_ref, g_ref, wq_ref, wk_ref, wv_ref, wf_ref, bf_ref, qg_ref, kg_ref, *rest):
    q_ref, k_ref, kb_ref, v_ref, vb_ref, lf_ref = rest[-6:]
    tm = x_ref.shape[0]
    h = _rms_norm(x_ref[...], g_ref[...]).astype(BF16)
    q = jnp.dot(h, wq_ref[...], preferred_element_type=F32)
    k = jnp.dot(h, wk_ref[...], preferred_element_type=F32)
    v = jnp.dot(h, wv_ref[...], preferred_element_type=F32)
    qg, kg = qg_ref[...], kg_ref[...]
    qscale = (FOX_DH ** -0.5) * LOG2E
    for hd in range(FOX_HEADS):
        sl = slice(hd * FOX_DH, (hd + 1) * FOX_DH)
        qn = _rms_norm(q[:, sl], qg)
        kn = _rms_norm(k[:, sl], kg)
        q_ref[:, sl] = (qn * qscale).astype(BF16)
        k_ref[0, pl.ds(hd, tm, stride=FOX_HEADS), :] = kn
        kb_ref[:, sl] = kn.astype(BF16)
        v_ref[0, pl.ds(hd, tm, stride=FOX_HEADS), :] = v[:, sl]
    vb_ref[...] = v.astype(BF16)
    f = jnp.dot(h, wf_ref[...], preferred_element_type=F32) + bf_ref[...]
    lf_ref[...] = (-_softplus(-f))[:, :FOX_HEADS]


def _in_odd(x, g, wq, wk, wv, wf, bf, qg, kg, slot, nslots, k_all, v_all):
    t, d = x.shape
    tm = min(TOKEN_TILE, t)
    w = wq.shape[1]
    row = lambda n: pl.BlockSpec((tm, n), lambda i: (i, 0))
    full = lambda a: pl.BlockSpec(a.shape, lambda i: (0, 0))
    stacked = pl.BlockSpec((1, tm * FOX_HEADS, FOX_DH), lambda i: (slot, i, 0))
    stacked_shape = jax.ShapeDtypeStruct((nslots, t * FOX_HEADS, FOX_DH), F32)
    args = [x, g, wq, wk, wv, wf, bf, qg, kg]
    in_specs = [row(d), full(g), full(wq), full(wk), full(wv), full(wf), full(bf),
                full(qg), full(kg)]
    aliases = {}
    if k_all is not None:
        aliases = {len(args): 1, len(args) + 1: 3}
        args += [k_all, v_all]
        in_specs += [pl.BlockSpec(memory_space=pl.ANY)] * 2
    return pl.pallas_call(
        _in_odd_kernel,
        grid=(t // tm,),
        in_specs=in_specs,
        out_specs=[row(w), stacked, row(w), stacked, row(w), row(FOX_HEADS)],
        out_shape=[jax.ShapeDtypeStruct((t, w), BF16), stacked_shape,
                   jax.ShapeDtypeStruct((t, w), BF16), stacked_shape,
                   jax.ShapeDtypeStruct((t, w), BF16),
                   jax.ShapeDtypeStruct((t, FOX_HEADS), F32)],
        input_output_aliases=aliases,
        compiler_params=_params(("parallel",)),
        name="in_odd",
    )(*args)


def _key_bias_kernel(lf_ref, o_ref, carry):
    @pl.when(pl.program_id(0) == 0)
    def _():
        carry[...] = jnp.zeros_like(carry)

    lf = lf_ref[...]
    tt = lf.shape[1]
    upper = (lax.broadcasted_iota(jnp.int32, (tt, tt), 0)
             <= lax.broadcasted_iota(jnp.int32, (tt, tt), 1)).astype(BF16)
    hi, mid, lo = _split3(lf)
    csum = (jnp.dot(hi, upper, preferred_element_type=F32)
            + jnp.dot(mid, upper, preferred_element_type=F32)
            + jnp.dot(lo, upper, preferred_element_type=F32)) + carry[:, 0:1]
    o_ref[...] = csum * (-LOG2E)
    carry[...] = jnp.broadcast_to(csum[:, tt - 1:tt], carry.shape)


def _key_bias(lf_t, tile):
    b, h, s = lf_t.shape
    tt = tile if s % tile == 0 else s
    return pl.pallas_call(
        _key_bias_kernel,
        grid=(s // tt,),
        in_specs=[pl.BlockSpec((b * h, tt), lambda j: (0, j))],
        out_specs=pl.BlockSpec((b * h, tt), lambda j: (0, j)),
        out_shape=jax.ShapeDtypeStruct((b * h, s), F32),
        scratch_shapes=[pltpu.VMEM((b * h, LANES), F32)],
        compiler_params=_params(("arbitrary",)),
        name="key_bias",
    )(lf_t.reshape(b * h, s)).reshape(b, h, s)


def _attn_prompt_kernel(qi_ref, kj_ref, q_ref, k_ref, v_ref, nb_ref, o_ref, m_ref, acc_ref):
    i, j = qi_ref[pl.program_id(1)], kj_ref[pl.program_id(1)]
    tq, tk = q_ref.shape[0], k_ref.shape[0]
    heads = range(FOX_HEADS)
    sls = [slice(h * FOX_DH, (h + 1) * FOX_DH) for h in heads]

    @pl.when(j == 0)
    def _():
        m_ref[...] = jnp.full_like(m_ref, NEG_BIG)
        acc_ref[...] = jnp.zeros_like(acc_ref)

    def step(masked):
        nb = nb_ref[0]
        ones = jnp.ones((tk, LANES), BF16)
        reps = tk // LANES
        if masked:
            causal = (lax.broadcasted_iota(jnp.int32, (tq, tk), 0)
                      >= lax.broadcasted_iota(jnp.int32, (tq, tk), 1))
        for h, sl in zip(heads, sls):
            s = lax.dot_general(q_ref[:, sl], k_ref[:, sl], (((1,), (1,)), ((), ())),
                                preferred_element_type=F32) + nb[h:h + 1, :]
            if masked:
                s = jnp.where(causal, s, NEG_BIG)
            m_prev = m_ref[h]
            m_new = jnp.maximum(m_prev, jnp.max(s, axis=-1, keepdims=True))
            alpha = jnp.exp2(m_prev - m_new)
            p = jnp.exp2(s - jnp.concatenate([m_new] * reps, axis=-1)).astype(BF16)
            pv = jnp.dot(p, jnp.concatenate([v_ref[:, sl], ones], axis=-1),
                         preferred_element_type=F32)
            acc_ref[h] = jnp.concatenate([alpha, alpha], axis=-1) * acc_ref[h] + pv
            m_ref[h] = m_new

    @pl.when(j < i)
    def _():
        step(False)

    @pl.when(j == i)
    def _():
        step(True)
        for h, sl in zip(heads, sls):
            acc = acc_ref[h]
            o_ref[:, sl] = (acc[:, :FOX_DH] / acc[:, FOX_DH:]).astype(o_ref.dtype)


def _attn_prompt(q, k, v, nbias, nseq):
    t, w = q.shape
    s = t // nseq
    tile = min(ATTN_TILE, s)
    nblk = s // tile
    pairs = [(i, j) for i in range(nblk) for j in range(i + 1)]
    qi = jnp.asarray([p[0] for p in pairs], jnp.int32)
    kj = jnp.asarray([p[1] for p in pairs], jnp.int32)
    q_spec = pl.BlockSpec((tile, w), lambda b, n, qi, kj: (b * nblk + qi[n], 0))
    kv_spec = pl.BlockSpec((tile, w), lambda b, n, qi, kj: (b * nblk + kj[n], 0))
    return pl.pallas_call(
        _attn_prompt_kernel,
        grid_spec=pltpu.PrefetchScalarGridSpec(
            num_scalar_prefetch=2,
            grid=(nseq, len(pairs)),
            in_specs=[q_spec, kv_spec, kv_spec,
                      pl.BlockSpec((1, FOX_HEADS, tile), lambda b, n, qi, kj: (b, 0, kj[n]))],
            out_specs=q_spec,
            scratch_shapes=[pltpu.VMEM((FOX_HEADS, tile, LANES), F32),
                            pltpu.VMEM((FOX_HEADS, tile, FOX_DH + LANES), F32)]),
        out_shape=jax.ShapeDtypeStruct((t, w), BF16),
        compiler_params=_params(("parallel", "arbitrary")),
        name="attn_prompt",
    )(qi, kj, q, k, v, nbias)


def _attn_sample_kernel(q_ref, kc_ref, vc_ref, kn_ref, vn_ref, nb_ref, o_ref):
    tq = q_ref.shape[0]
    past = kc_ref.shape[2] // FOX_HEADS
    nb = nb_ref[0]
    heads = range(FOX_HEADS)
    sls = [slice(h * FOX_DH, (h + 1) * FOX_DH) for h in heads]
    nt = (((1,), (1,)), ((), ()))
    causal = (lax.broadcasted_iota(jnp.int32, (tq, tq), 0)
              >= lax.broadcasted_iota(jnp.int32, (tq, tq), 1))
    kc = [kc_ref[0, 0, pl.ds(h, past, stride=FOX_HEADS), :].astype(BF16) for h in heads]
    s_c = [lax.dot_general(q_ref[:, sl], x, nt, preferred_element_type=F32) + nb[h:h + 1, :past]
           for h, sl, x in zip(heads, sls, kc)]
    s_n = [jnp.where(causal,
                     lax.dot_general(q_ref[:, sl], kn_ref[:, sl], nt,
                                     preferred_element_type=F32) + nb[h:h + 1, past:], NEG_BIG)
           for h, sl in zip(heads, sls)]
    m = [jnp.maximum(jnp.max(a, axis=-1, keepdims=True), jnp.max(b, axis=-1, keepdims=True))
         for a, b in zip(s_c, s_n)]
    p_c = [jnp.exp2(a - mm) for a, mm in zip(s_c, m)]
    p_n = [jnp.exp2(b - mm) for b, mm in zip(s_n, m)]
    l = [jnp.sum(a, axis=-1, keepdims=True) + jnp.sum(b, axis=-1, keepdims=True)
         for a, b in zip(p_c, p_n)]
    vc = [vc_ref[0, 0, pl.ds(h, past, stride=FOX_HEADS), :].astype(BF16) for h in heads]
    o = [jnp.dot(a.astype(BF16), x, preferred_element_type=F32)
         + jnp.dot(b.astype(BF16), vn_ref[:, sl], preferred_element_type=F32)
         for a, b, x, sl in zip(p_c, p_n, vc, sls)]
    for h, sl in zip(heads, sls):
        o_ref[:, sl] = (o[h] / l[h]).astype(o_ref.dtype)


def _attn_sample(q, k_new, v_new, k_cache, v_cache, layer, nbias, nseq):
    t, w = q.shape
    tq = t // nseq
    past = k_cache.shape[2] // FOX_HEADS
    row = pl.BlockSpec((tq, w), lambda b: (b, 0))
    cache = pl.BlockSpec((1, 1) + k_cache.shape[2:], lambda b: (layer, b, 0, 0))
    return pl.pallas_call(
        _attn_sample_kernel,
        grid=(nseq,),
        in_specs=[row, cache, cache, row, row,
                  pl.BlockSpec((1, FOX_HEADS, past + tq), lambda b: (b, 0, 0))],
        out_specs=row,
        out_shape=jax.ShapeDtypeStruct((t, w), BF16),
        compiler_params=_params(("parallel",)),
        name="attn_sample",
    )(q, k_cache, v_cache, k_new, v_new, nbias)


def _pad_lanes(w):
    return jnp.pad(w, ((0, 0), (0, LANES - w.shape[1])))


def _row(v, width=None):
    v = v.reshape(1, -1).astype(F32)
    if width is not None:
        v = jnp.pad(v, ((0, 0), (0, width - v.shape[1])))
    return v


def _prep_weights(norm_mix_g, norm_ffn_g, w_in_even, gdn_conv_w, gdn_a_log, gdn_dt_bias,
                  gdn_norm_g, sconv_w, w_out_even, w_in_odd, fox_b_f, fox_q_norm_g,
                  fox_k_norm_g, w_out_odd, ffn_w_gate, ffn_w_up, ffn_w_down):
    depth = norm_mix_g.shape[0]
    nqkv = gdn_conv_w.shape[2]
    nz = GDN_HEADS * GDN_DV
    nsc = sconv_w.shape[2]
    o1, o2 = nqkv, nqkv + nz
    o4 = o2 + 2 * GDN_HEADS
    o5, o6 = o4 + nsc, o4 + 2 * nsc
    fw = FOX_HEADS * FOX_DH
    shared = dict(g_ffn=norm_ffn_g.astype(F32).reshape(depth, 1, -1),
                  wg=ffn_w_gate.astype(BF16), wu=ffn_w_up.astype(BF16),
                  wd=ffn_w_down.astype(BF16),
                  wout_even=w_out_even.astype(BF16), wout_odd=w_out_odd.astype(BF16))
    layers = []
    for layer in range(depth):
        i = layer // 2
        p = dict(g_mix=_row(norm_mix_g[layer]))
        if layer % 2 == 0:
            w = w_in_even[i]
            p.update(wqkv=w[:, :o1].astype(BF16), wz=w[:, o1:o2].astype(BF16),
                     wab=_pad_lanes(w[:, o2:o4]).astype(BF16),
                     wgb=w[:, o4:o5].astype(BF16), wgc=w[:, o5:o6].astype(BF16),
                     whin=w[:, o6:].astype(BF16),
                     convw=gdn_conv_w[i].astype(F32), alog=_row(gdn_a_log[i], LANES),
                     dtb=_row(gdn_dt_bias[i], LANES), ng=_row(gdn_norm_g[i]),
                     scw=sconv_w[i].astype(F32))
        else:
            w = w_in_odd[i]
            p.update(wq=w[:, :fw].astype(BF16), wk=w[:, fw:2 * fw].astype(BF16),
                     wv=w[:, 2 * fw:3 * fw].astype(BF16),
                     wf=_pad_lanes(w[:, 3 * fw:]).astype(BF16),
                     bf=_row(fox_b_f[i], LANES), qg=_row(fox_q_norm_g[i]),
                     kg=_row(fox_k_norm_g[i]))
        layers.append(p)
    return layers, shared


def _trunk(x, layers, shared, gdn_conv, gdn_s, sconv, fox_k, fox_v, fox_logf):
    nseq, seq, d = x.shape
    x = x.reshape(nseq * seq, d)
    has_past = fox_k is not None
    if has_past:
        fox_k = fox_k.reshape(fox_k.shape[:2] + (-1, FOX_DH))
        fox_v = fox_v.reshape(fox_v.shape[:2] + (-1, FOX_DH))
    new_lf, new_s, new_conv, new_sc = [], [], [], []
    k_all = v_all = None
    for layer, p in enumerate(layers):
        i = layer // 2
        if layer % 2 == 0:
            if has_past:
                conv0, s0, sc0 = gdn_conv[i], gdn_s[i], sconv[i]
            else:
                conv0 = jnp.zeros((nseq, GDN_CONV - 1, p["wqkv"].shape[1]), F32)
                s0 = jnp.zeros((nseq, GDN_HEADS, GDN_DK, GDN_DV), F32)
                sc0 = jnp.zeros((nseq, SC_CONV - 1, p["wgb"].shape[1]), F32)
            y, z, ab, mix_b, conv_fin, sc_fin = _in_even(
                x, p["g_mix"], p["wqkv"], p["wz"], p["wab"], p["wgb"], p["wgc"], p["whin"],
                conv0, sc0, p["convw"], p["scw"], nseq)
            mix_a, s_fin = _gdn(y, z, ab, s0, p["alog"], p["dtb"], p["ng"])
            wout = shared["wout_even"]
            new_s.append(s_fin)
            new_conv.append(conv_fin)
            new_sc.append(sc_fin)
        else:
            q, k_all, kb, v_all, vb, lf = _in_odd(x, p["g_mix"], p["wq"], p["wk"], p["wv"],
                                                  p["wf"], p["bf"], p["qg"], p["kg"], i,
                                                  len(layers) // 2, k_all, v_all)
            lf = lf.reshape(nseq, seq, FOX_HEADS)
            if has_past:
                lf_all = jnp.concatenate([fox_logf[i].astype(F32), lf], axis=1)
                nbias = _key_bias(lf_all.transpose(0, 2, 1), ATTN_TILE)
                mix_a = _attn_sample(q, kb, vb, fox_k, fox_v, i, nbias, nseq)
            else:
                nbias = _key_bias(lf.transpose(0, 2, 1), ATTN_TILE)
                mix_a = _attn_prompt(q, kb, vb, nbias, nseq)
            mix_b = None
            wout = shared["wout_odd"]
            new_lf.append(lf)
        x = _out_ffn(x, mix_a, mix_b, wout, i, shared["g_ffn"], shared["wg"], shared["wu"],
                     shared["wd"], layer)
    kv_shape = (len(layers) // 2, nseq, seq, FOX_HEADS, FOX_DH)
    return (x.reshape(nseq, seq, d), k_all.reshape(kv_shape), v_all.reshape(kv_shape),
            jnp.stack(new_lf), jnp.stack(new_s), jnp.stack(new_conv), jnp.stack(new_sc))


def kernel(x_prompt, x_sample, cache_fox_k, cache_fox_v, cache_fox_logf, state_gdn_S,
           state_gdn_conv, state_sconv, norm_mix_g, norm_ffn_g, w_in_even, gdn_conv_w,
           gdn_a_log, gdn_dt_bias, gdn_norm_g, sconv_w, w_out_even, w_in_odd, fox_b_f,
           fox_q_norm_g, fox_k_norm_g, w_out_odd, ffn_w_gate, ffn_w_up, ffn_w_down):
    layers, shared = _prep_weights(
        norm_mix_g, norm_ffn_g, w_in_even, gdn_conv_w, gdn_a_log, gdn_dt_bias, gdn_norm_g,
        sconv_w, w_out_even, w_in_odd, fox_b_f, fox_q_norm_g, fox_k_norm_g, w_out_odd,
        ffn_w_gate, ffn_w_up, ffn_w_down)
    (y_prompt, p_fox_k, p_fox_v, p_fox_logf, p_gdn_s, p_gdn_conv, p_sconv) = _trunk(
        x_prompt, layers, shared, None, None, None, None, None, None)
    (y_sample, s_fox_k, s_fox_v, s_fox_logf, s_gdn_s, s_gdn_conv, s_sconv) = _trunk(
        x_sample, layers, shared, state_gdn_conv, state_gdn_S, state_sconv, cache_fox_k,
        cache_fox_v, cache_fox_logf)
    return (y_prompt, y_sample, p_fox_k, p_fox_v, p_fox_logf, p_gdn_s, p_gdn_conv, p_sconv,
            s_fox_k, s_fox_v, s_fox_logf, s_gdn_s, s_gdn_conv, s_sconv)
```

```python
import functools

import jax
import jax.numpy as jnp
from jax import lax
from jax.experimental import pallas as pl
from jax.experimental.pallas import tpu as pltpu

F32 = jnp.float32
BF16 = jnp.bfloat16

EPS = 1e-6
CHUNK = 64
GDN_HEADS = 4
GDN_DK = 128
GDN_DV = 128
GDN_CONV = 4
SC_CONV = 3
FOX_HEADS = 8
FOX_DH = 128
LANES = 128
SUBLANES = 8
INV_BLOCK = 16
LOG2E = 1.4426950408889634
NEG_BIG = -1e30
VMEM_LIMIT_BYTES = 56 * 1024 * 1024

TOKEN_TILE = 512
FFN_TOKEN_TILE = 512
MXU_COLS = 256
ATTN_TILE = 512
ATTN_Q_BLOCKS = 2
GDN_BLOCKS_PER_STEP = 8


def _params(semantics):
    return pltpu.CompilerParams(dimension_semantics=semantics,
                                vmem_limit_bytes=VMEM_LIMIT_BYTES)


def _mm(a, b):
    return jnp.dot(a.astype(BF16), b.astype(BF16), preferred_element_type=F32)


def _mm_nt(a, b):
    return lax.dot_general(a.astype(BF16), b.astype(BF16), (((1,), (1,)), ((), ())),
                           preferred_element_type=F32)


def _split3(x):
    hi = x.astype(BF16)
    r1 = x - hi.astype(F32)
    mid = r1.astype(BF16)
    lo = (r1 - mid.astype(F32)).astype(BF16)
    return hi, mid, lo


def _rms_norm(x, g):
    return x * lax.rsqrt(jnp.mean(x * x, axis=-1, keepdims=True) + EPS) * g


def _softplus(x):
    return jnp.maximum(x, 0.0) + jnp.log1p(jnp.exp(-jnp.abs(x)))


def _sigmoid(x):
    return 1.0 / (1.0 + jnp.exp(-x))


def _silu(x):
    return x * _sigmoid(x)


def _in_even_kernel(ns, rows, x_ref, g_ref, wqkv_ref, wz_ref, wab_ref, wgb_ref, wgc_ref,
                    whin_ref, conv0_ref, sc0_ref, convw_ref, scw_ref,
                    y_ref, z_ref, ab_ref, ob_ref, conv_ref, sc_ref, xslab, uslab):
    top = SUBLANES
    kc, ks = GDN_CONV - 1, SC_CONV - 1

    @pl.when(pl.program_id(1) == 0)
    def _():
        conv_ref[...] = conv0_ref[...]
        sc_ref[...] = sc0_ref[...]

    x = x_ref[...].reshape(ns * rows, x_ref.shape[2])
    h = _rms_norm(x, g_ref[...]).astype(BF16)
    cw = convw_ref[...]
    sw = scw_ref[...]

    def causal_conv(t, c0, slab, tail_ref, w, taps, s):
        for j in range(t.shape[1] // LANES):
            cs = slice(c0 + j * LANES, c0 + (j + 1) * LANES)
            k = cs.start // LANES
            ts = t[s * rows:(s + 1) * rows, j * LANES:(j + 1) * LANES]
            slab[s, k, top - taps:top, :] = tail_ref[s, :, cs]
            slab[s, k, top:top + rows, :] = ts
            tail_ref[s, :, cs] = ts[rows - taps:rows]
            acc = ts * w[taps:taps + 1, cs]
            for d in range(1, taps + 1):
                acc = acc + slab[s, k, top - d:top - d + rows, :] * w[taps - d:taps - d + 1, cs]
            yield cs, j, acc

    def dot(w_ref, cols):
        return jnp.dot(h, w_ref[:, cols], preferred_element_type=F32)

    jobs = []
    for c in range(0, wqkv_ref.shape[1], MXU_COLS):
        def conv_silu(t, c=c):
            for s in range(ns):
                for cs, _, acc in causal_conv(t, c, xslab, conv_ref, cw, kc, s):
                    y_ref[s, :, cs] = _silu(acc)
        jobs.append((lambda c=c: dot(wqkv_ref, slice(c, c + MXU_COLS)), conv_silu))
    for c in range(0, wgb_ref.shape[1], MXU_COLS):
        def gated_conv(t, c=c):
            gate_c, h_in, gate_b = t
            u = gate_c * h_in
            for s in range(ns):
                for cs, j, acc in causal_conv(u, c, uslab, sc_ref, sw, ks, s):
                    gb = gate_b[s * rows:(s + 1) * rows, j * LANES:(j + 1) * LANES]
                    ob_ref[s, :, cs] = (gb * acc).astype(ob_ref.dtype)
        cols = slice(c, c + MXU_COLS)
        jobs.append((lambda cols=cols: (dot(wgc_ref, cols), dot(whin_ref, cols), dot(wgb_ref, cols)),
                     gated_conv))

    def store_z(t):
        z_ref[...] = t.astype(BF16).reshape(z_ref.shape)

    def store_ab(t):
        ab_ref[...] = t.reshape(ab_ref.shape)
    jobs.append((lambda: jnp.dot(h, wz_ref[...], preferred_element_type=F32), store_z))
    jobs.append((lambda: jnp.dot(h, wab_ref[...], preferred_element_type=F32), store_ab))

    nxt = jobs[0][0]()
    for n, (_, post) in enumerate(jobs):
        cur = nxt
        if n + 1 < len(jobs):
            nxt = jobs[n + 1][0]()
        post(cur)


def _stream_blocking(nseq, seq, rows_per_step):
    rows = min(seq, rows_per_step)
    ns = max(1, min(nseq, rows_per_step // rows))
    return ns, rows


def _in_even(x, g, wqkv, wz, wab, wgb, wgc, whin, conv0, sc0, convw, scw, nseq):
    t, d = x.shape
    seq = t // nseq
    ns, rows = _stream_blocking(nseq, seq, TOKEN_TILE)
    nqkv, nz, nsc = wqkv.shape[1], wz.shape[1], wgb.shape[1]
    row = lambda n: pl.BlockSpec((ns, rows, n), lambda b, c: (b, c, 0))
    seq3 = lambda a: pl.BlockSpec((ns,) + a.shape[1:], lambda b, c: (b, 0, 0))
    full = lambda a: pl.BlockSpec(a.shape, lambda b, c: (0, 0))
    y, z, ab, ob, conv_fin, sc_fin = pl.pallas_call(
        functools.partial(_in_even_kernel, ns, rows),
        grid=(nseq // ns, seq // rows),
        in_specs=[row(d), full(g), full(wqkv), full(wz), full(wab), full(wgb), full(wgc),
                  full(whin), seq3(conv0), seq3(sc0), full(convw), full(scw)],
        out_specs=[row(nqkv), row(nz), row(LANES), row(nsc), seq3(conv0), seq3(sc0)],
        out_shape=[jax.ShapeDtypeStruct((nseq, seq, nqkv), F32),
                   jax.ShapeDtypeStruct((nseq, seq, nz), BF16),
                   jax.ShapeDtypeStruct((nseq, seq, LANES), F32),
                   jax.ShapeDtypeStruct((nseq, seq, nsc), BF16),
                   jax.ShapeDtypeStruct(conv0.shape, F32),
                   jax.ShapeDtypeStruct(sc0.shape, F32)],
        scratch_shapes=[pltpu.VMEM((ns, nqkv // LANES, SUBLANES + rows, LANES), F32),
                        pltpu.VMEM((ns, nsc // LANES, SUBLANES + rows, LANES), F32)],
        compiler_params=_params(("parallel", "arbitrary")),
        name="in_even",
    )(x.reshape(nseq, seq, d), g, wqkv, wz, wab, wgb, wgc, whin, conv0, sc0, convw, scw)
    return y, z, ab, ob.reshape(t, nsc), conv_fin, sc_fin


def _unit_lower_inverse_many(a_list):
    c = a_list[0].shape[0]
    ii = lax.broadcasted_iota(jnp.int32, (c, c), 0)
    jj = lax.broadcasted_iota(jnp.int32, (c, c), 1)
    eye = (ii == jj).astype(F32)
    same_block = (ii // INV_BLOCK) == (jj // INV_BLOCK)
    a0 = [jnp.where(same_block, a, 0.0) for a in a_list]
    a1 = [a - z for a, z in zip(a_list, a0)]
    d_inv = [eye - z for z in a0]
    power = a0
    for _ in range(INV_BLOCK.bit_length() - 2):
        power = [_mm(p, p) for p in power]
        d_inv = [d + _mm(d, p) for d, p in zip(d_inv, power)]
    n = [_mm(d, r) for d, r in zip(d_inv, a1)]
    m = [eye - x for x in n]
    power = n
    for _ in range(max((c // INV_BLOCK).bit_length() - 2, 0)):
        power = [_mm(p, p) for p in power]
        m = [x + _mm(x, p) for x, p in zip(m, power)]
    return [_mm(x, d) for x, d in zip(m, d_inv)]


def _gdn_kernel(ns, nc, y_ref, z_ref, ab_ref, s0_ref, alog_ref, dtb_ref, ng_ref, *rest):
    out_ref, s_ref = rest[-2:]
    ch = CHUNK
    nk = GDN_HEADS * GDN_DK

    @pl.when(pl.program_id(1) == 0)
    def _():
        s_ref[...] = s0_ref[...]

    alog, dtb, ng = alog_ref[...], dtb_ref[...], ng_ref[...]
    ii = lax.broadcasted_iota(jnp.int32, (ch, ch), 0)
    jj = lax.broadcasted_iota(jnp.int32, (ch, ch), 1)
    incl = ii >= jj
    strict = ii > jj
    tri = incl.astype(BF16)

    blocks = [(s, k) for s in range(ns) for k in range(nc)]
    chains = [(b, h) for b in range(len(blocks)) for h in range(GDN_HEADS)]

    ys, gcums, gcum_ts, betas = [], [], [], []
    for s, k in blocks:
        ys.append(y_ref[s, k * ch:(k + 1) * ch, :])
        ab = ab_ref[s, k * ch:(k + 1) * ch, :]
        g = -jnp.exp(alog) * _softplus(ab + dtb)
        betas.append(_sigmoid(ab))
        g_hi, g_mid, g_lo = _split3(g)
        gcum = (jnp.dot(tri, g_hi, preferred_element_type=F32)
                + jnp.dot(tri, g_mid, preferred_element_type=F32)
                + jnp.dot(tri, g_lo, preferred_element_type=F32))
        gcums.append(gcum)
        gcum_ts.append(gcum.T)

    def l2n(x):
        return x * lax.rsqrt(jnp.sum(x * x, axis=-1, keepdims=True) + EPS)

    q = [l2n(ys[b][:, h * GDN_DK:(h + 1) * GDN_DK]) * (GDN_DK ** -0.5) for b, h in chains]
    kk = [l2n(ys[b][:, nk + h * GDN_DK:nk + (h + 1) * GDN_DK]) for b, h in chains]
    v = [ys[b][:, 2 * nk + h * GDN_DV:2 * nk + (h + 1) * GDN_DV] for b, h in chains]
    gc = [gcums[b][:, h:h + 1] for b, h in chains]
    gr = [gcum_ts[b][h:h + 1, :] for b, h in chains]
    g_last = [gcums[b][ch - 1:ch, h:h + 1] for b, h in chains]
    bh = [betas[b][:, GDN_HEADS + h:GDN_HEADS + h + 1] for b, h in chains]
    decay = [jnp.exp(jnp.where(incl, c_ - r_, -jnp.inf)) for c_, r_ in zip(gc, gr)]
    kb = [x * b_ for x, b_ in zip(kk, bh)]
    a = [_mm_nt(x, y_) * jnp.where(strict, d, 0.0) for x, y_, d in zip(kb, kk, decay)]
    t_inv = _unit_lower_inverse_many(a)
    eg = [jnp.exp(c_) for c_ in gc]
    rhs = [jnp.concatenate([v_ * b_, x * e], axis=-1) for v_, b_, x, e in zip(v, bh, kb, eg)]
    sol = [_mm(t, r).astype(BF16) for t, r in zip(t_inv, rhs)]
    p = [(_mm_nt(x, y_) * d).astype(BF16) for x, y_, d in zip(q, kk, decay)]
    kd = [(x * jnp.exp(l - c_)).astype(BF16) for x, l, c_ in zip(kk, g_last, gc)]
    ksol = [lax.dot_general(x, y_, (((0,), (0,)), ((), ())), preferred_element_type=F32)
            for x, y_ in zip(kd, sol)]
    psol = [jnp.dot(x, y_, preferred_element_type=F32) for x, y_ in zip(p, sol)]
    lhs = [jnp.concatenate([-ks_[:, GDN_DV:], x * e - ps_[:, GDN_DV:]], axis=0).astype(BF16)
           for ks_, x, e, ps_ in zip(ksol, q, eg, psol)]
    sdecay = [jnp.exp(l) for l in g_last]

    state = [[s_ref[s, h] for h in range(GDN_HEADS)] for s in range(ns)]
    for k in range(nc):
        idx = [(s, h, (s * nc + k) * GDN_HEADS + h) for s in range(ns) for h in range(GDN_HEADS)]
        r = [jnp.dot(lhs[c], state[s][h].astype(BF16), preferred_element_type=F32)
             for s, h, c in idx]
        for (s, h, c), r_ in zip(idx, r):
            state[s][h] = state[s][h] * sdecay[c] + r_[:GDN_DK] + ksol[c][:, :GDN_DV]
        for (s, h, c), r_ in zip(idx, r):
            o = r_[GDN_DK:] + psol[c][:, :GDN_DV]
            zh = z_ref[s, k * ch:(k + 1) * ch, h * GDN_DV:(h + 1) * GDN_DV].astype(F32)
            o = _rms_norm(o, ng) * _silu(zh)
            out_ref[s, k * ch:(k + 1) * ch, h * GDN_DV:(h + 1) * GDN_DV] = o.astype(out_ref.dtype)
    for s in range(ns):
        for h in range(GDN_HEADS):
            s_ref[s, h] = state[s][h]


def _gdn(y, z, ab, s_in, li, alog, dtb, ng, slot, nslots, s_all):
    nseq, seq, nqkv = y.shape
    nz = z.shape[2]
    nchunk = seq // CHUNK
    ns = min(nseq, GDN_BLOCKS_PER_STEP)
    nc = max(1, min(nchunk, GDN_BLOCKS_PER_STEP // ns))
    rows = nc * CHUNK
    row = lambda n: pl.BlockSpec((ns, rows, n), lambda b, c: (b, c, 0))
    state = lambda layer: pl.BlockSpec((None, ns) + s_in.shape[2:],
                                       lambda b, c: (layer, b, 0, 0, 0))
    full = lambda a: pl.BlockSpec(a.shape, lambda b, c: (0, 0))
    args = [y, z, ab, s_in, alog, dtb, ng]
    in_specs = [row(nqkv), row(nz), row(LANES), state(li), full(alog), full(dtb), full(ng)]
    aliases = {}
    if s_all is not None:
        aliases = {len(args): 1}
        args.append(s_all)
        in_specs.append(pl.BlockSpec(memory_space=pl.ANY))
    o, s_all = pl.pallas_call(
        functools.partial(_gdn_kernel, ns, nc),
        grid=(nseq // ns, nchunk // nc),
        in_specs=in_specs,
        out_specs=[row(nz), state(slot)],
        out_shape=[jax.ShapeDtypeStruct((nseq, seq, nz), BF16),
                   jax.ShapeDtypeStruct((nslots,) + s_in.shape[1:], F32)],
        input_output_aliases=aliases,
        compiler_params=_params(("parallel", "arbitrary")),
        name="gdn",
    )(*args)
    return o.reshape(nseq * seq, nz), s_all


def _out_ffn_kernel(x_ref, ma_ref, mb_ref, wa_ref, wb_ref, g_ref, wg_ref, wu_ref, wd_ref, o_ref,
                    act):
    xn = (x_ref[...] + jnp.dot(ma_ref[...], wa_ref[0], preferred_element_type=F32)
          + jnp.dot(mb_ref[...], wb_ref[0], preferred_element_type=F32))
    h = _rms_norm(xn, g_ref[0]).astype(BF16)
    for c in range(0, wg_ref.shape[2], MXU_COLS):
        gate = jnp.dot(h, wg_ref[0, :, c:c + MXU_COLS], preferred_element_type=F32)
        up = jnp.dot(h, wu_ref[0, :, c:c + MXU_COLS], preferred_element_type=F32)
        act[:, c:c + MXU_COLS] = (_silu(gate) * up).astype(BF16)
    o_ref[...] = xn + jnp.dot(act[...], wd_ref[0], preferred_element_type=F32)


def _out_ffn(x, mix_a, mix_b, wout, iw, g, wg, wu, wd, layer):
    t, d = x.shape
    tm = min(FFN_TOKEN_TILE, t)
    dff = wg.shape[2]
    half = wout.shape[1] // 2
    row = lambda a: pl.BlockSpec((tm, a.shape[1]), lambda i: (i, 0))
    once = lambda shape, imap: pl.BlockSpec(shape, imap, pipeline_mode=pl.Buffered(1))
    if mix_b is None:
        mix_b = mix_a
        ma_spec = pl.BlockSpec((tm, half), lambda i: (i, 0))
        mb_spec = pl.BlockSpec((tm, half), lambda i: (i, 1))
    else:
        ma_spec, mb_spec = row(mix_a), row(mix_b)
    return pl.pallas_call(
        _out_ffn_kernel,
        grid=(t // tm,),
        in_specs=[row(x), ma_spec, mb_spec,
                  once((1, half, d), lambda i: (iw, 0, 0)),
                  once((1, half, d), lambda i: (iw, 1, 0)),
                  once((1, 1, d), lambda i: (layer, 0, 0)),
                  once((1, d, dff), lambda i: (layer, 0, 0)),
                  once((1, d, dff), lambda i: (layer, 0, 0)),
                  once((1, dff, d), lambda i: (layer, 0, 0))],
        out_specs=row(x),
        out_shape=jax.ShapeDtypeStruct((t, d), F32),
        scratch_shapes=[pltpu.VMEM((tm, dff), BF16)],
        compiler_params=_params(("parallel",)),
        name="out_ffn",
    )(x, mix_a, mix_b, wout, wout, g, wg, wu, wd)


def _in_odd_kernel(x_ref, g_ref, wq_ref, wk_ref, wv_ref, wf_ref, bf_ref, qg_ref, kg_ref, *rest):
    q_ref, k_ref, kb_ref, v_ref, vb_ref, lf_ref = rest[-6:]
    tm = x_ref.shape[0]
    h = _rms_norm(x_ref[...], g_ref[...]).astype(BF16)
    q = jnp.dot(h, wq_ref[...], preferred_element_type=F32)
    k = jnp.dot(h, wk_ref[...], preferred_element_type=F32)
    v = jnp.dot(h, wv_ref[...], preferred_element_type=F32)
    qg, kg = qg_ref[...], kg_ref[...]
    qscale = (FOX_DH ** -0.5) * LOG2E
    for hd in range(FOX_HEADS):
        sl = slice(hd * FOX_DH, (hd + 1) * FOX_DH)
        qn = _rms_norm(q[:, sl], qg)
        kn = _rms_norm(k[:, sl], kg)
        q_ref[:, sl] = (qn * qscale).astype(BF16)
        k_ref[0, pl.ds(hd, tm, stride=FOX_HEADS), :] = kn
        kb_ref[:, sl] = kn.astype(BF16)
        v_ref[0, pl.ds(hd, tm, stride=FOX_HEADS), :] = v[:, sl]
    vb_ref[...] = v.astype(BF16)
    f = jnp.dot(h, wf_ref[...], preferred_element_type=F32) + bf_ref[...]
    lf_ref[...] = (-_softplus(-f))[:, :FOX_HEADS]


def _in_odd(x, g, wq, wk, wv, wf, bf, qg, kg, slot, nslots, k_all, v_all):
    t, d = x.shape
    tm = min(TOKEN_TILE, t)
    w = wq.shape[1]
    row = lambda n: pl.BlockSpec((tm, n), lambda i: (i, 0))
    full = lambda a: pl.BlockSpec(a.shape, lambda i: (0, 0))
    stacked = pl.BlockSpec((1, tm * FOX_HEADS, FOX_DH), lambda i: (slot, i, 0))
    stacked_shape = jax.ShapeDtypeStruct((nslots, t * FOX_HEADS, FOX_DH), F32)
    args = [x, g, wq, wk, wv, wf, bf, qg, kg]
    in_specs = [row(d), full(g), full(wq), full(wk), full(wv), full(wf), full(bf),
                full(qg), full(kg)]
    aliases = {}
    if k_all is not None:
        aliases = {len(args): 1, len(args) + 1: 3}
        args += [k_all, v_all]
        in_specs += [pl.BlockSpec(memory_space=pl.ANY)] * 2
    return pl.pallas_call(
        _in_odd_kernel,
        grid=(t // tm,),
        in_specs=in_specs,
        out_specs=[row(w), stacked, row(w), stacked, row(w), row(FOX_HEADS)],
        out_shape=[jax.ShapeDtypeStruct((t, w), BF16), stacked_shape,
                   jax.ShapeDtypeStruct((t, w), BF16), stacked_shape,
                   jax.ShapeDtypeStruct((t, w), BF16),
                   jax.ShapeDtypeStruct((t, FOX_HEADS), F32)],
        input_output_aliases=aliases,
        compiler_params=_params(("parallel",)),
        name="in_odd",
    )(*args)


def _key_bias_kernel(lf_ref, o_ref, carry):
    @pl.when(pl.program_id(0) == 0)
    def _():
        carry[...] = jnp.zeros_like(carry)

    lf = lf_ref[...]
    tt = lf.shape[1]
    upper = (lax.broadcasted_iota(jnp.int32, (tt, tt), 0)
             <= lax.broadcasted_iota(jnp.int32, (tt, tt), 1)).astype(BF16)
    hi, mid, lo = _split3(lf)
    csum = (jnp.dot(hi, upper, preferred_element_type=F32)
            + jnp.dot(mid, upper, preferred_element_type=F32)
            + jnp.dot(lo, upper, preferred_element_type=F32)) + carry[:, 0:1]
    o_ref[...] = csum * (-LOG2E)
    carry[...] = jnp.broadcast_to(csum[:, tt - 1:tt], carry.shape)


def _key_bias(lf_t, tile):
    b, h, s = lf_t.shape
    tt = tile if s % tile == 0 else s
    return pl.pallas_call(
        _key_bias_kernel,
        grid=(s // tt,),
        in_specs=[pl.BlockSpec((b * h, tt), lambda j: (0, j))],
        out_specs=pl.BlockSpec((b * h, tt), lambda j: (0, j)),
        out_shape=jax.ShapeDtypeStruct((b * h, s), F32),
        scratch_shapes=[pltpu.VMEM((b * h, LANES), F32)],
        compiler_params=_params(("arbitrary",)),
        name="key_bias",
    )(lf_t.reshape(b * h, s)).reshape(b, h, s)


def _attn_prompt_kernel(ratio, qi_ref, kj_ref, q_ref, k_ref, v_ref, nb_ref, o_ref, m_ref, acc_ref):
    i, j = qi_ref[pl.program_id(1)], kj_ref[pl.program_id(1)]
    tk = k_ref.shape[0]
    heads = range(FOX_HEADS)
    sls = [slice(h * FOX_DH, (h + 1) * FOX_DH) for h in heads]

    @pl.when(j == 0)
    def _():
        m_ref[...] = jnp.full_like(m_ref, NEG_BIG)
        acc_ref[...] = jnp.zeros_like(acc_ref)

    def step(row0, masked):
        nr = q_ref.shape[0] - row0
        nb = nb_ref[0]
        ones = jnp.ones((tk, LANES), BF16)
        reps = tk // LANES
        if masked:
            causal = (lax.broadcasted_iota(jnp.int32, (nr, tk), 0)
                      >= lax.broadcasted_iota(jnp.int32, (nr, tk), 1))
        rows = slice(row0, q_ref.shape[0])
        for h, sl in zip(heads, sls):
            s = lax.dot_general(q_ref[rows, sl], k_ref[:, sl], (((1,), (1,)), ((), ())),
                                preferred_element_type=F32) + nb[h:h + 1, :]
            if masked:
                s = jnp.where(causal, s, NEG_BIG)
            m_prev = m_ref[h, rows, :]
            m_new = jnp.maximum(m_prev, jnp.max(s, axis=-1, keepdims=True))
            alpha = jnp.exp2(m_prev - m_new)
            p = jnp.exp2(s - jnp.concatenate([m_new] * reps, axis=-1)).astype(BF16)
            pv = jnp.dot(p, jnp.concatenate([v_ref[:, sl], ones], axis=-1),
                         preferred_element_type=F32)
            acc_ref[h, rows, :] = jnp.concatenate([alpha, alpha], axis=-1) * acc_ref[h, rows, :] + pv
            m_ref[h, rows, :] = m_new

    @pl.when(j < ratio * i)
    def _():
        step(0, False)

    for sub in range(ratio):
        @pl.when(j == ratio * i + sub)
        def _(sub=sub):
            step(sub * tk, True)
            if sub == ratio - 1:
                for h, sl in zip(heads, sls):
                    acc = acc_ref[h]
                    o_ref[:, sl] = (acc[:, :FOX_DH] / acc[:, FOX_DH:]).astype(o_ref.dtype)


def _attn_prompt(q, k, v, nbias, nseq):
    t, w = q.shape
    s = t // nseq
    tk = min(ATTN_TILE, s)
    ratio = ATTN_Q_BLOCKS if s % (ATTN_Q_BLOCKS * tk) == 0 else 1
    tq = ratio * tk
    nq, nk = s // tq, s // tk
    pairs = [(i, j) for i in range(nq) for j in range(ratio * (i + 1))]
    qi = jnp.asarray([p[0] for p in pairs], jnp.int32)
    kj = jnp.asarray([p[1] for p in pairs], jnp.int32)
    q_spec = pl.BlockSpec((tq, w), lambda b, n, qi, kj: (b * nq + qi[n], 0))
    kv_spec = pl.BlockSpec((tk, w), lambda b, n, qi, kj: (b * nk + kj[n], 0))
    return pl.pallas_call(
        functools.partial(_attn_prompt_kernel, ratio),
        grid_spec=pltpu.PrefetchScalarGridSpec(
            num_scalar_prefetch=2,
            grid=(nseq, len(pairs)),
            in_specs=[q_spec, kv_spec, kv_spec,
                      pl.BlockSpec((1, FOX_HEADS, tk), lambda b, n, qi, kj: (b, 0, kj[n]))],
            out_specs=q_spec,
            scratch_shapes=[pltpu.VMEM((FOX_HEADS, tq, LANES), F32),
                            pltpu.VMEM((FOX_HEADS, tq, FOX_DH + LANES), F32)]),
        out_shape=jax.ShapeDtypeStruct((t, w), BF16),
        compiler_params=_params(("parallel", "arbitrary")),
        name="attn_prompt",
    )(qi, kj, q, k, v, nbias)


def _attn_sample_kernel(q_ref, kc_ref, vc_ref, kn_ref, vn_ref, nb_ref, o_ref):
    tq = q_ref.shape[0]
    past = kc_ref.shape[2] // FOX_HEADS
    nb = nb_ref[0]
    heads = range(FOX_HEADS)
    sls = [slice(h * FOX_DH, (h + 1) * FOX_DH) for h in heads]
    nt = (((1,), (1,)), ((), ()))
    causal = (lax.broadcasted_iota(jnp.int32, (tq, tq), 0)
              >= lax.broadcasted_iota(jnp.int32, (tq, tq), 1))
    kc = [kc_ref[0, 0, pl.ds(h, past, stride=FOX_HEADS), :].astype(BF16) for h in heads]
    s_c = [lax.dot_general(q_ref[:, sl], x, nt, preferred_element_type=F32) + nb[h:h + 1, :past]
           for h, sl, x in zip(heads, sls, kc)]
    s_n = [jnp.where(causal,
                     lax.dot_general(q_ref[:, sl], kn_ref[:, sl], nt,
                                     preferred_element_type=F32) + nb[h:h + 1, past:], NEG_BIG)
           for h, sl in zip(heads, sls)]
    m = [jnp.maximum(jnp.max(a, axis=-1, keepdims=True), jnp.max(b, axis=-1, keepdims=True))
         for a, b in zip(s_c, s_n)]
    p_c = [jnp.exp2(a - mm) for a, mm in zip(s_c, m)]
    p_n = [jnp.exp2(b - mm) for b, mm in zip(s_n, m)]
    l = [jnp.sum(a, axis=-1, keepdims=True) + jnp.sum(b, axis=-1, keepdims=True)
         for a, b in zip(p_c, p_n)]
    vc = [vc_ref[0, 0, pl.ds(h, past, stride=FOX_HEADS), :].astype(BF16) for h in heads]
    o = [jnp.dot(a.astype(BF16), x, preferred_element_type=F32)
         + jnp.dot(b.astype(BF16), vn_ref[:, sl], preferred_element_type=F32)
         for a, b, x, sl in zip(p_c, p_n, vc, sls)]
    for h, sl in zip(heads, sls):
        o_ref[:, sl] = (o[h] / l[h]).astype(o_ref.dtype)


def _attn_sample(q, k_new, v_new, k_cache, v_cache, layer, nbias, nseq):
    t, w = q.shape
    tq = t // nseq
    past = k_cache.shape[2] // FOX_HEADS
    row = pl.BlockSpec((tq, w), lambda b: (b, 0))
    cache = pl.BlockSpec((1, 1) + k_cache.shape[2:], lambda b: (layer, b, 0, 0))
    return pl.pallas_call(
        _attn_sample_kernel,
        grid=(nseq,),
        in_specs=[row, cache, cache, row, row,
                  pl.BlockSpec((1, FOX_HEADS, past + tq), lambda b: (b, 0, 0))],
        out_specs=row,
        out_shape=jax.ShapeDtypeStruct((t, w), BF16),
        compiler_params=_params(("parallel",)),
        name="attn_sample",
    )(q, k_cache, v_cache, k_new, v_new, nbias)


def _pad_lanes(w):
    return jnp.pad(w, ((0, 0), (0, LANES - w.shape[1])))


def _row(v, width=None):
    v = v.reshape(1, -1).astype(F32)
    if width is not None:
        v = jnp.pad(v, ((0, 0), (0, width - v.shape[1])))
    return v


def _prep_weights(norm_mix_g, norm_ffn_g, w_in_even, gdn_conv_w, gdn_a_log, gdn_dt_bias,
                  gdn_norm_g, sconv_w, w_out_even, w_in_odd, fox_b_f, fox_q_norm_g,
                  fox_k_norm_g, w_out_odd, ffn_w_gate, ffn_w_up, ffn_w_down):
    depth = norm_mix_g.shape[0]
    nqkv = gdn_conv_w.shape[2]
    nz = GDN_HEADS * GDN_DV
    nsc = sconv_w.shape[2]
    o1, o2 = nqkv, nqkv + nz
    o4 = o2 + 2 * GDN_HEADS
    o5, o6 = o4 + nsc, o4 + 2 * nsc
    fw = FOX_HEADS * FOX_DH
    shared = dict(g_ffn=norm_ffn_g.astype(F32).reshape(depth, 1, -1),
                  wg=ffn_w_gate.astype(BF16), wu=ffn_w_up.astype(BF16),
                  wd=ffn_w_down.astype(BF16),
                  wout_even=w_out_even.astype(BF16), wout_odd=w_out_odd.astype(BF16))
    layers = []
    for layer in range(depth):
        i = layer // 2
        p = dict(g_mix=_row(norm_mix_g[layer]))
        if layer % 2 == 0:
            w = w_in_even[i]
            p.update(wqkv=w[:, :o1].astype(BF16), wz=w[:, o1:o2].astype(BF16),
                     wab=_pad_lanes(w[:, o2:o4]).astype(BF16),
                     wgb=w[:, o4:o5].astype(BF16), wgc=w[:, o5:o6].astype(BF16),
                     whin=w[:, o6:].astype(BF16),
                     convw=gdn_conv_w[i].astype(F32), alog=_row(gdn_a_log[i], LANES),
                     dtb=_row(gdn_dt_bias[i], LANES), ng=_row(gdn_norm_g[i]),
                     scw=sconv_w[i].astype(F32))
        else:
            w = w_in_odd[i]
            p.update(wq=w[:, :fw].astype(BF16), wk=w[:, fw:2 * fw].astype(BF16),
                     wv=w[:, 2 * fw:3 * fw].astype(BF16),
                     wf=_pad_lanes(w[:, 3 * fw:]).astype(BF16),
                     bf=_row(fox_b_f[i], LANES), qg=_row(fox_q_norm_g[i]),
                     kg=_row(fox_k_norm_g[i]))
        layers.append(p)
    return layers, shared


def _trunk(x, layers, shared, gdn_conv, gdn_s, sconv, fox_k, fox_v, fox_logf):
    nseq, seq, d = x.shape
    x = x.reshape(nseq * seq, d)
    has_past = fox_k is not None
    if has_past:
        fox_k = fox_k.reshape(fox_k.shape[:2] + (-1, FOX_DH))
        fox_v = fox_v.reshape(fox_v.shape[:2] + (-1, FOX_DH))
    new_lf, new_conv, new_sc = [], [], []
    k_all = v_all = s_all = None
    n_even = (len(layers) + 1) // 2
    for layer, p in enumerate(layers):
        i = layer // 2
        if layer % 2 == 0:
            if has_past:
                conv0, s_in, li, sc0 = gdn_conv[i], gdn_s, i, sconv[i]
            else:
                conv0 = jnp.zeros((nseq, GDN_CONV - 1, p["wqkv"].shape[1]), F32)
                s_in, li = jnp.zeros((1, nseq, GDN_HEADS, GDN_DK, GDN_DV), F32), 0
                sc0 = jnp.zeros((nseq, SC_CONV - 1, p["wgb"].shape[1]), F32)
            y, z, ab, mix_b, conv_fin, sc_fin = _in_even(
                x, p["g_mix"], p["wqkv"], p["wz"], p["wab"], p["wgb"], p["wgc"], p["whin"],
                conv0, sc0, p["convw"], p["scw"], nseq)
            mix_a, s_all = _gdn(y, z, ab, s_in, li, p["alog"], p["dtb"], p["ng"], i, n_even, s_all)
            wout = shared["wout_even"]
            new_conv.append(conv_fin)
            new_sc.append(sc_fin)
        else:
            q, k_all, kb, v_all, vb, lf = _in_odd(x, p["g_mix"], p["wq"], p["wk"], p["wv"],
                                                  p["wf"], p["bf"], p["qg"], p["kg"], i,
                                                  len(layers) // 2, k_all, v_all)
            lf = lf.reshape(nseq, seq, FOX_HEADS)
            if has_past:
                lf_all = jnp.concatenate([fox_logf[i].astype(F32), lf], axis=1)
                nbias = _key_bias(lf_all.transpose(0, 2, 1), ATTN_TILE)
                mix_a = _attn_sample(q, kb, vb, fox_k, fox_v, i, nbias, nseq)
            else:
                nbias = _key_bias(lf.transpose(0, 2, 1), ATTN_TILE)
                mix_a = _attn_prompt(q, kb, vb, nbias, nseq)
            mix_b = None
            wout = shared["wout_odd"]
            new_lf.append(lf)
        x = _out_ffn(x, mix_a, mix_b, wout, i, shared["g_ffn"], shared["wg"], shared["wu"],
                     shared["wd"], layer)
    kv_shape = (len(layers) // 2, nseq, seq, FOX_HEADS, FOX_DH)
    return (x.reshape(nseq, seq, d), k_all.reshape(kv_shape), v_all.reshape(kv_shape),
            jnp.stack(new_lf), s_all, jnp.stack(new_conv), jnp.stack(new_sc))


def kernel(x_prompt, x_sample, cache_fox_k, cache_fox_v, cache_fox_logf, state_gdn_S,
           state_gdn_conv, state_sconv, norm_mix_g, norm_ffn_g, w_in_even, gdn_conv_w,
           gdn_a_log, gdn_dt_bias, gdn_norm_g, sconv_w, w_out_even, w_in_odd, fox_b_f,
           fox_q_norm_g, fox_k_norm_g, w_out_odd, ffn_w_gate, ffn_w_up, ffn_w_down):
    layers, shared = _prep_weights(
        norm_mix_g, norm_ffn_g, w_in_even, gdn_conv_w, gdn_a_log, gdn_dt_bias, gdn_norm_g,
        sconv_w, w_out_even, w_in_odd, fox_b_f, fox_q_norm_g, fox_k_norm_g, w_out_odd,
        ffn_w_gate, ffn_w_up, ffn_w_down)
    (y_prompt, p_fox_k, p_fox_v, p_fox_logf, p_gdn_s, p_gdn_conv, p_sconv) = _trunk(
        x_prompt, layers, shared, None, None, None, None, None, None)
    (y_sample, s_fox_k, s_fox_v, s_fox_logf, s_gdn_s, s_gdn_conv, s_sconv) = _trunk(
        x_sample, layers, shared, state_gdn_conv, state_gdn_S, state_sconv, cache_fox_k,
        cache_fox_v, cache_fox_logf)
    return (y_prompt, y_sample, p_fox_k, p_fox_v, p_fox_logf, p_gdn_s, p_gdn_conv, p_sconv,
            s_fox_k, s_fox_v, s_fox_logf, s_gdn_s, s_gdn_conv, s_sconv)
```

```python
import functools

import jax
import jax.numpy as jnp
from jax import lax
from jax.experimental import pallas as pl
from jax.experimental.pallas import tpu as pltpu

F32 = jnp.float32
BF16 = jnp.bfloat16

EPS = 1e-6
CHUNK = 64
GDN_HEADS = 4
GDN_DK = 128
GDN_DV = 128
GDN_CONV = 4
SC_CONV = 3
FOX_HEADS = 8
FOX_DH = 128
LANES = 128
SUBLANES = 8
INV_BLOCK = 16
LOG2E = 1.4426950408889634
NEG_BIG = -1e30
VMEM_LIMIT_BYTES = 56 * 1024 * 1024

TOKEN_TILE = 512
FFN_TOKEN_TILE = 512
MXU_COLS = 256
ATTN_TILE = 512
ATTN_Q_BLOCKS = 2
GDN_BLOCKS_PER_STEP = 8


def _params(semantics):
    return pltpu.CompilerParams(dimension_semantics=semantics,
                                vmem_limit_bytes=VMEM_LIMIT_BYTES)


def _mm(a, b):
    return jnp.dot(a.astype(BF16), b.astype(BF16), preferred_element_type=F32)


def _mm_nt(a, b):
    return lax.dot_general(a.astype(BF16), b.astype(BF16), (((1,), (1,)), ((), ())),
                           preferred_element_type=F32)


def _split3(x):
    hi = x.astype(BF16)
    r1 = x - hi.astype(F32)
    mid = r1.astype(BF16)
    lo = (r1 - mid.astype(F32)).astype(BF16)
    return hi, mid, lo


def _rms_norm(x, g):
    return x * lax.rsqrt(jnp.mean(x * x, axis=-1, keepdims=True) + EPS) * g


def _softplus(x):
    return jnp.maximum(x, 0.0) + jnp.log1p(jnp.exp(-jnp.abs(x)))


def _sigmoid(x):
    return 1.0 / (1.0 + jnp.exp(-x))


def _silu(x):
    return x * _sigmoid(x)


def _in_even_kernel(ns, rows, x_ref, g_ref, wqkv_ref, wz_ref, wab_ref, wgb_ref, wgc_ref,
                    whin_ref, conv0_ref, sc0_ref, convw_ref, scw_ref,
                    y_ref, z_ref, ab_ref, ob_ref, conv_ref, sc_ref, xslab, uslab):
    top = SUBLANES
    kc, ks = GDN_CONV - 1, SC_CONV - 1

    @pl.when(pl.program_id(1) == 0)
    def _():
        conv_ref[...] = conv0_ref[...]
        sc_ref[...] = sc0_ref[...]

    x = x_ref[...].reshape(ns * rows, x_ref.shape[2])
    h = _rms_norm(x, g_ref[...]).astype(BF16)
    cw = convw_ref[...]
    sw = scw_ref[...]

    def causal_conv(t, c0, slab, tail_ref, w, taps, s):
        for j in range(t.shape[1] // LANES):
            cs = slice(c0 + j * LANES, c0 + (j + 1) * LANES)
            k = cs.start // LANES
            ts = t[s * rows:(s + 1) * rows, j * LANES:(j + 1) * LANES]
            slab[s, k, top - taps:top, :] = tail_ref[s, :, cs]
            slab[s, k, top:top + rows, :] = ts
            tail_ref[s, :, cs] = ts[rows - taps:rows]
            acc = ts * w[taps:taps + 1, cs]
            for d in range(1, taps + 1):
                acc = acc + slab[s, k, top - d:top - d + rows, :] * w[taps - d:taps - d + 1, cs]
            yield cs, j, acc

    def dot(w_ref, cols):
        return jnp.dot(h, w_ref[:, cols], preferred_element_type=F32)

    jobs = []
    for c in range(0, wqkv_ref.shape[1], MXU_COLS):
        def conv_silu(t, c=c):
            for s in range(ns):
                for cs, _, acc in causal_conv(t, c, xslab, conv_ref, cw, kc, s):
                    y_ref[s, :, cs] = _silu(acc)
        jobs.append((lambda c=c: dot(wqkv_ref, slice(c, c + MXU_COLS)), conv_silu))
    for c in range(0, wgb_ref.shape[1], MXU_COLS):
        def gated_conv(t, c=c):
            gate_c, h_in, gate_b = t
            u = gate_c * h_in
            for s in range(ns):
                for cs, j, acc in causal_conv(u, c, uslab, sc_ref, sw, ks, s):
                    gb = gate_b[s * rows:(s + 1) * rows, j * LANES:(j + 1) * LANES]
                    ob_ref[s, :, cs] = (gb * acc).astype(ob_ref.dtype)
        cols = slice(c, c + MXU_COLS)
        jobs.append((lambda cols=cols: (dot(wgc_ref, cols), dot(whin_ref, cols), dot(wgb_ref, cols)),
                     gated_conv))

    def store_z(t):
        z_ref[...] = t.astype(BF16).reshape(z_ref.shape)

    def store_ab(t):
        ab_ref[...] = t.reshape(ab_ref.shape)
    jobs.append((lambda: jnp.dot(h, wz_ref[...], preferred_element_type=F32), store_z))
    jobs.append((lambda: jnp.dot(h, wab_ref[...], preferred_element_type=F32), store_ab))

    nxt = jobs[0][0]()
    for n, (_, post) in enumerate(jobs):
        cur = nxt
        if n + 1 < len(jobs):
            nxt = jobs[n + 1][0]()
        post(cur)


def _stream_blocking(nseq, seq, rows_per_step):
    rows = min(seq, rows_per_step)
    ns = max(1, min(nseq, rows_per_step // rows))
    return ns, rows


def _in_even(x, g, wqkv, wz, wab, wgb, wgc, whin, conv0, sc0, convw, scw, nseq):
    t, d = x.shape
    seq = t // nseq
    ns, rows = _stream_blocking(nseq, seq, TOKEN_TILE)
    nqkv, nz, nsc = wqkv.shape[1], wz.shape[1], wgb.shape[1]
    row = lambda n: pl.BlockSpec((ns, rows, n), lambda b, c: (b, c, 0))
    seq3 = lambda a: pl.BlockSpec((ns,) + a.shape[1:], lambda b, c: (b, 0, 0))
    full = lambda a: pl.BlockSpec(a.shape, lambda b, c: (0, 0))
    y, z, ab, ob, conv_fin, sc_fin = pl.pallas_call(
        functools.partial(_in_even_kernel, ns, rows),
        grid=(nseq // ns, seq // rows),
        in_specs=[row(d), full(g), full(wqkv), full(wz), full(wab), full(wgb), full(wgc),
                  full(whin), seq3(conv0), seq3(sc0), full(convw), full(scw)],
        out_specs=[row(nqkv), row(nz), row(LANES), row(nsc), seq3(conv0), seq3(sc0)],
        out_shape=[jax.ShapeDtypeStruct((nseq, seq, nqkv), F32),
                   jax.ShapeDtypeStruct((nseq, seq, nz), BF16),
                   jax.ShapeDtypeStruct((nseq, seq, LANES), F32),
                   jax.ShapeDtypeStruct((nseq, seq, nsc), BF16),
                   jax.ShapeDtypeStruct(conv0.shape, F32),
                   jax.ShapeDtypeStruct(sc0.shape, F32)],
        scratch_shapes=[pltpu.VMEM((ns, nqkv // LANES, SUBLANES + rows, LANES), F32),
                        pltpu.VMEM((ns, nsc // LANES, SUBLANES + rows, LANES), F32)],
        compiler_params=_params(("parallel", "arbitrary")),
        name="in_even",
    )(x.reshape(nseq, seq, d), g, wqkv, wz, wab, wgb, wgc, whin, conv0, sc0, convw, scw)
    return y, z, ab, ob.reshape(t, nsc), conv_fin, sc_fin


def _unit_lower_inverse_many(a_list):
    c = a_list[0].shape[0]
    ii = lax.broadcasted_iota(jnp.int32, (c, c), 0)
    jj = lax.broadcasted_iota(jnp.int32, (c, c), 1)
    eye = (ii == jj).astype(F32)
    same_block = (ii // INV_BLOCK) == (jj // INV_BLOCK)
    a0 = [jnp.where(same_block, a, 0.0) for a in a_list]
    a1 = [a - z for a, z in zip(a_list, a0)]
    d_inv = [eye - z for z in a0]
    power = a0
    for _ in range(INV_BLOCK.bit_length() - 2):
        power = [_mm(p, p) for p in power]
        d_inv = [d + _mm(d, p) for d, p in zip(d_inv, power)]
    n = [_mm(d, r) for d, r in zip(d_inv, a1)]
    m = [eye - x for x in n]
    power = n
    for _ in range(max((c // INV_BLOCK).bit_length() - 2, 0)):
        power = [_mm(p, p) for p in power]
        m = [x + _mm(x, p) for x, p in zip(m, power)]
    return [_mm(x, d) for x, d in zip(m, d_inv)]


def _gdn_kernel(ns, nc, y_ref, z_ref, ab_ref, s0_ref, alog_ref, dtb_ref, ng_ref, *rest):
    out_ref, s_ref = rest[-2:]
    ch = CHUNK
    nk = GDN_HEADS * GDN_DK

    @pl.when(pl.program_id(1) == 0)
    def _():
        s_ref[...] = s0_ref[...]

    alog, dtb, ng = alog_ref[...], dtb_ref[...], ng_ref[...]
    ii = lax.broadcasted_iota(jnp.int32, (ch, ch), 0)
    jj = lax.broadcasted_iota(jnp.int32, (ch, ch), 1)
    incl = ii >= jj
    strict = ii > jj
    tri = incl.astype(BF16)

    blocks = [(s, k) for s in range(ns) for k in range(nc)]
    chains = [(b, h) for b in range(len(blocks)) for h in range(GDN_HEADS)]

    ys, gcums, gcum_ts, betas = [], [], [], []
    for s, k in blocks:
        ys.append(y_ref[s, k * ch:(k + 1) * ch, :])
        ab = ab_ref[s, k * ch:(k + 1) * ch, :]
        g = -jnp.exp(alog) * _softplus(ab + dtb)
        betas.append(_sigmoid(ab))
        g_hi, g_mid, g_lo = _split3(g)
        gcum = (jnp.dot(tri, g_hi, preferred_element_type=F32)
                + jnp.dot(tri, g_mid, preferred_element_type=F32)
                + jnp.dot(tri, g_lo, preferred_element_type=F32))
        gcums.append(gcum)
        gcum_ts.append(gcum.T)

    def l2n(x):
        return x * lax.rsqrt(jnp.sum(x * x, axis=-1, keepdims=True) + EPS)

    q = [l2n(ys[b][:, h * GDN_DK:(h + 1) * GDN_DK]) * (GDN_DK ** -0.5) for b, h in chains]
    kk = [l2n(ys[b][:, nk + h * GDN_DK:nk + (h + 1) * GDN_DK]) for b, h in chains]
    v = [ys[b][:, 2 * nk + h * GDN_DV:2 * nk + (h + 1) * GDN_DV] for b, h in chains]
    gc = [gcums[b][:, h:h + 1] for b, h in chains]
    gr = [gcum_ts[b][h:h + 1, :] for b, h in chains]
    g_last = [gcums[b][ch - 1:ch, h:h + 1] for b, h in chains]
    bh = [betas[b][:, GDN_HEADS + h:GDN_HEADS + h + 1] for b, h in chains]
    decay = [jnp.exp(jnp.where(incl, c_ - r_, -jnp.inf)) for c_, r_ in zip(gc, gr)]
    kb = [x * b_ for x, b_ in zip(kk, bh)]
    a = [_mm_nt(x, y_) * jnp.where(strict, d, 0.0) for x, y_, d in zip(kb, kk, decay)]
    t_inv = _unit_lower_inverse_many(a)
    eg = [jnp.exp(c_) for c_ in gc]
    rhs = [jnp.concatenate([v_ * b_, x * e], axis=-1) for v_, b_, x, e in zip(v, bh, kb, eg)]
    sol = [_mm(t, r).astype(BF16) for t, r in zip(t_inv, rhs)]
    p = [(_mm_nt(x, y_) * d).astype(BF16) for x, y_, d in zip(q, kk, decay)]
    kd = [(x * jnp.exp(l - c_)).astype(BF16) for x, l, c_ in zip(kk, g_last, gc)]
    ksol = [lax.dot_general(x, y_, (((0,), (0,)), ((), ())), preferred_element_type=F32)
            for x, y_ in zip(kd, sol)]
    psol = [jnp.dot(x, y_, preferred_element_type=F32) for x, y_ in zip(p, sol)]
    lhs = [jnp.concatenate([-ks_[:, GDN_DV:], x * e - ps_[:, GDN_DV:]], axis=0).astype(BF16)
           for ks_, x, e, ps_ in zip(ksol, q, eg, psol)]
    sdecay = [jnp.exp(l) for l in g_last]

    state = [[s_ref[s, h] for h in range(GDN_HEADS)] for s in range(ns)]
    for k in range(nc):
        idx = [(s, h, (s * nc + k) * GDN_HEADS + h) for s in range(ns) for h in range(GDN_HEADS)]
        r = [jnp.dot(lhs[c], state[s][h].astype(BF16), preferred_element_type=F32)
             for s, h, c in idx]
        for (s, h, c), r_ in zip(idx, r):
            state[s][h] = state[s][h] * sdecay[c] + r_[:GDN_DK] + ksol[c][:, :GDN_DV]
        for (s, h, c), r_ in zip(idx, r):
            o = r_[GDN_DK:] + psol[c][:, :GDN_DV]
            zh = z_ref[s, k * ch:(k + 1) * ch, h * GDN_DV:(h + 1) * GDN_DV].astype(F32)
            o = _rms_norm(o, ng) * _silu(zh)
            out_ref[s, k * ch:(k + 1) * ch, h * GDN_DV:(h + 1) * GDN_DV] = o.astype(out_ref.dtype)
    for s in range(ns):
        for h in range(GDN_HEADS):
            s_ref[s, h] = state[s][h]


def _gdn(y, z, ab, s_in, li, alog, dtb, ng, slot, nslots, s_all):
    nseq, seq, nqkv = y.shape
    nz = z.shape[2]
    nchunk = seq // CHUNK
    ns = min(nseq, GDN_BLOCKS_PER_STEP)
    nc = max(1, min(nchunk, GDN_BLOCKS_PER_STEP // ns))
    rows = nc * CHUNK
    row = lambda n: pl.BlockSpec((ns, rows, n), lambda b, c: (b, c, 0))
    state = lambda layer: pl.BlockSpec((None, ns) + s_in.shape[2:],
                                       lambda b, c: (layer, b, 0, 0, 0))
    full = lambda a: pl.BlockSpec(a.shape, lambda b, c: (0, 0))
    args = [y, z, ab, s_in, alog, dtb, ng]
    in_specs = [row(nqkv), row(nz), row(LANES), state(li), full(alog), full(dtb), full(ng)]
    aliases = {}
    if s_all is not None:
        aliases = {len(args): 1}
        args.append(s_all)
        in_specs.append(pl.BlockSpec(memory_space=pl.ANY))
    o, s_all = pl.pallas_call(
        functools.partial(_gdn_kernel, ns, nc),
        grid=(nseq // ns, nchunk // nc),
        in_specs=in_specs,
        out_specs=[row(nz), state(slot)],
        out_shape=[jax.ShapeDtypeStruct((nseq, seq, nz), BF16),
                   jax.ShapeDtypeStruct((nslots,) + s_in.shape[1:], F32)],
        input_output_aliases=aliases,
        compiler_params=_params(("parallel", "arbitrary")),
        name="gdn",
    )(*args)
    return o.reshape(nseq * seq, nz), s_all


def _out_ffn_kernel(nfirst, *refs):
    (x0, a0, b0, x1, a1, b1, wa_ref, wb_ref, g_ref, wg_ref, wu_ref, wd_ref, o0, o1, act) = refs

    def tile(x_ref, ma_ref, mb_ref, o_ref):
        xn = (x_ref[...] + jnp.dot(ma_ref[...], wa_ref[0], preferred_element_type=F32)
              + jnp.dot(mb_ref[...], wb_ref[0], preferred_element_type=F32))
        h = _rms_norm(xn, g_ref[0]).astype(BF16)
        for c in range(0, wg_ref.shape[2], MXU_COLS):
            gate = jnp.dot(h, wg_ref[0, :, c:c + MXU_COLS], preferred_element_type=F32)
            up = jnp.dot(h, wu_ref[0, :, c:c + MXU_COLS], preferred_element_type=F32)
            act[:, c:c + MXU_COLS] = (_silu(gate) * up).astype(BF16)
        o_ref[...] = xn + jnp.dot(act[...], wd_ref[0], preferred_element_type=F32)

    @pl.when(pl.program_id(0) < nfirst)
    def _():
        tile(x0, a0, b0, o0)

    @pl.when(pl.program_id(0) >= nfirst)
    def _():
        tile(x1, a1, b1, o1)


def _out_ffn(xs, mixes, wout, iw, g, wg, wu, wd, layer):
    d = xs[0].shape[1]
    tm = min(FFN_TOKEN_TILE, xs[0].shape[0], xs[1].shape[0])
    n0, n1 = xs[0].shape[0] // tm, xs[1].shape[0] // tm
    dff = wg.shape[2]
    half = wout.shape[1] // 2
    imaps = [lambda i: jnp.minimum(i, n0 - 1), lambda i: jnp.maximum(i - n0, 0)]
    once = lambda shape, imap: pl.BlockSpec(shape, imap, pipeline_mode=pl.Buffered(1))
    args, in_specs, out_specs, out_shape = [], [], [], []
    for x, (ma, mb), im in zip(xs, mixes, imaps):
        col_b = 0
        if mb is None:
            mb, col_b = ma, 1
        args += [x, ma, mb]
        in_specs += [pl.BlockSpec((tm, d), lambda i, im=im: (im(i), 0)),
                     pl.BlockSpec((tm, half), lambda i, im=im: (im(i), 0)),
                     pl.BlockSpec((tm, half), lambda i, im=im, c=col_b: (im(i), c))]
        out_specs.append(pl.BlockSpec((tm, d), lambda i, im=im: (im(i), 0)))
        out_shape.append(jax.ShapeDtypeStruct(x.shape, F32))
    args += [wout, wout, g, wg, wu, wd]
    in_specs += [once((1, half, d), lambda i: (iw, 0, 0)),
                 once((1, half, d), lambda i: (iw, 1, 0)),
                 once((1, 1, d), lambda i: (layer, 0, 0)),
                 once((1, d, dff), lambda i: (layer, 0, 0)),
                 once((1, d, dff), lambda i: (layer, 0, 0)),
                 once((1, dff, d), lambda i: (layer, 0, 0))]
    return pl.pallas_call(
        functools.partial(_out_ffn_kernel, n0),
        grid=(n0 + n1,),
        in_specs=in_specs,
        out_specs=out_specs,
        out_shape=out_shape,
        scratch_shapes=[pltpu.VMEM((tm, dff), BF16)],
        compiler_params=_params(("arbitrary",)),
        name="out_ffn",
    )(*args)


def _in_odd_kernel(x_ref, g_ref, wq_ref, wk_ref, wv_ref, wf_ref, bf_ref, qg_ref, kg_ref, *rest):
    q_ref, k_ref, kb_ref, v_ref, vb_ref, lf_ref = rest[-6:]
    tm = x_ref.shape[0]
    h = _rms_norm(x_ref[...], g_ref[...]).astype(BF16)
    q = jnp.dot(h, wq_ref[...], preferred_element_type=F32)
    k = jnp.dot(h, wk_ref[...], preferred_element_type=F32)
    v = jnp.dot(h, wv_ref[...], preferred_element_type=F32)
    qg, kg = qg_ref[...], kg_ref[...]
    qscale = (FOX_DH ** -0.5) * LOG2E
    for hd in range(FOX_HEADS):
        sl = slice(hd * FOX_DH, (hd + 1) * FOX_DH)
        qn = _rms_norm(q[:, sl], qg)
        kn = _rms_norm(k[:, sl], kg)
        q_ref[:, sl] = (qn * qscale).astype(BF16)
        k_ref[0, pl.ds(hd, tm, stride=FOX_HEADS), :] = kn
        kb_ref[:, sl] = kn.astype(BF16)
        v_ref[0, pl.ds(hd, tm, stride=FOX_HEADS), :] = v[:, sl]
    vb_ref[...] = v.astype(BF16)
    f = jnp.dot(h, wf_ref[...], preferred_element_type=F32) + bf_ref[...]
    lf_ref[...] = (-_softplus(-f))[:, :FOX_HEADS]


def _in_odd(x, g, wq, wk, wv, wf, bf, qg, kg, slot, nslots, k_all, v_all):
    t, d = x.shape
    tm = min(TOKEN_TILE, t)
    w = wq.shape[1]
    row = lambda n: pl.BlockSpec((tm, n), lambda i: (i, 0))
    full = lambda a: pl.BlockSpec(a.shape, lambda i: (0, 0))
    stacked = pl.BlockSpec((1, tm * FOX_HEADS, FOX_DH), lambda i: (slot, i, 0))
    stacked_shape = jax.ShapeDtypeStruct((nslots, t * FOX_HEADS, FOX_DH), F32)
    args = [x, g, wq, wk, wv, wf, bf, qg, kg]
    in_specs = [row(d), full(g), full(wq), full(wk), full(wv), full(wf), full(bf),
                full(qg), full(kg)]
    aliases = {}
    if k_all is not None:
        aliases = {len(args): 1, len(args) + 1: 3}
        args += [k_all, v_all]
        in_specs += [pl.BlockSpec(memory_space=pl.ANY)] * 2
    return pl.pallas_call(
        _in_odd_kernel,
        grid=(t // tm,),
        in_specs=in_specs,
        out_specs=[row(w), stacked, row(w), stacked, row(w), row(FOX_HEADS)],
        out_shape=[jax.ShapeDtypeStruct((t, w), BF16), stacked_shape,
                   jax.ShapeDtypeStruct((t, w), BF16), stacked_shape,
                   jax.ShapeDtypeStruct((t, w), BF16),
                   jax.ShapeDtypeStruct((t, FOX_HEADS), F32)],
        input_output_aliases=aliases,
        compiler_params=_params(("parallel",)),
        name="in_odd",
    )(*args)


def _key_bias_kernel(lf_ref, o_ref, carry):
    @pl.when(pl.program_id(0) == 0)
    def _():
        carry[...] = jnp.zeros_like(carry)

    lf = lf_ref[...]
    tt = lf.shape[1]
    upper = (lax.broadcasted_iota(jnp.int32, (tt, tt), 0)
             <= lax.broadcasted_iota(jnp.int32, (tt, tt), 1)).astype(BF16)
    hi, mid, lo = _split3(lf)
    csum = (jnp.dot(hi, upper, preferred_element_type=F32)
            + jnp.dot(mid, upper, preferred_element_type=F32)
            + jnp.dot(lo, upper, preferred_element_type=F32)) + carry[:, 0:1]
    o_ref[...] = csum * (-LOG2E)
    carry[...] = jnp.broadcast_to(csum[:, tt - 1:tt], carry.shape)


def _key_bias(lf_t, tile):
    b, h, s = lf_t.shape
    tt = tile if s % tile == 0 else s
    return pl.pallas_call(
        _key_bias_kernel,
        grid=(s // tt,),
        in_specs=[pl.BlockSpec((b * h, tt), lambda j: (0, j))],
        out_specs=pl.BlockSpec((b * h, tt), lambda j: (0, j)),
        out_shape=jax.ShapeDtypeStruct((b * h, s), F32),
        scratch_shapes=[pltpu.VMEM((b * h, LANES), F32)],
        compiler_params=_params(("arbitrary",)),
        name="key_bias",
    )(lf_t.reshape(b * h, s)).reshape(b, h, s)


def _attn_prompt_kernel(ratio, qi_ref, kj_ref, q_ref, k_ref, v_ref, nb_ref, o_ref, m_ref, acc_ref):
    i, j = qi_ref[pl.program_id(1)], kj_ref[pl.program_id(1)]
    tk = k_ref.shape[0] // ratio
    heads = range(FOX_HEADS)
    sls = [slice(h * FOX_DH, (h + 1) * FOX_DH) for h in heads]

    @pl.when(j == 0)
    def _():
        m_ref[...] = jnp.full_like(m_ref, NEG_BIG)
        acc_ref[...] = jnp.zeros_like(acc_ref)

    def step(sub, row0, masked):
        nr = q_ref.shape[0] - row0
        keys = slice(sub * tk, (sub + 1) * tk)
        nb = nb_ref[0, :, keys]
        ones = jnp.ones((tk, LANES), BF16)
        reps = tk // LANES
        if masked:
            causal = (lax.broadcasted_iota(jnp.int32, (nr, tk), 0)
                      >= lax.broadcasted_iota(jnp.int32, (nr, tk), 1))
        rows = slice(row0, q_ref.shape[0])
        for h, sl in zip(heads, sls):
            s = lax.dot_general(q_ref[rows, sl], k_ref[keys, sl], (((1,), (1,)), ((), ())),
                                preferred_element_type=F32) + nb[h:h + 1, :]
            if masked:
                s = jnp.where(causal, s, NEG_BIG)
            m_prev = m_ref[h, rows, :]
            m_new = jnp.maximum(m_prev, jnp.max(s, axis=-1, keepdims=True))
            alpha = jnp.exp2(m_prev - m_new)
            p = jnp.exp2(s - jnp.concatenate([m_new] * reps, axis=-1)).astype(BF16)
            pv = jnp.dot(p, jnp.concatenate([v_ref[keys, sl], ones], axis=-1),
                         preferred_element_type=F32)
            acc_ref[h, rows, :] = jnp.concatenate([alpha, alpha], axis=-1) * acc_ref[h, rows, :] + pv
            m_ref[h, rows, :] = m_new

    @pl.when(j < i)
    def _():
        for sub in range(ratio):
            step(sub, 0, False)

    @pl.when(j == i)
    def _():
        for sub in range(ratio):
            step(sub, sub * tk, True)
        for h, sl in zip(heads, sls):
            acc = acc_ref[h]
            o_ref[:, sl] = (acc[:, :FOX_DH] / acc[:, FOX_DH:]).astype(o_ref.dtype)


def _attn_prompt(q, k, v, nbias, nseq):
    t, w = q.shape
    s = t // nseq
    tk = min(ATTN_TILE, s)
    ratio = ATTN_Q_BLOCKS if s % (ATTN_Q_BLOCKS * tk) == 0 else 1
    tq = ratio * tk
    nblk = s // tq
    pairs = [(i, j) for i in range(nblk) for j in range(i + 1)]
    qi = jnp.asarray([p[0] for p in pairs], jnp.int32)
    kj = jnp.asarray([p[1] for p in pairs], jnp.int32)
    q_spec = pl.BlockSpec((tq, w), lambda b, n, qi, kj: (b * nblk + qi[n], 0))
    kv_spec = pl.BlockSpec((tq, w), lambda b, n, qi, kj: (b * nblk + kj[n], 0))
    return pl.pallas_call(
        functools.partial(_attn_prompt_kernel, ratio),
        grid_spec=pltpu.PrefetchScalarGridSpec(
            num_scalar_prefetch=2,
            grid=(nseq, len(pairs)),
            in_specs=[q_spec, kv_spec, kv_spec,
                      pl.BlockSpec((1, FOX_HEADS, tq), lambda b, n, qi, kj: (b, 0, kj[n]))],
            out_specs=q_spec,
            scratch_shapes=[pltpu.VMEM((FOX_HEADS, tq, LANES), F32),
                            pltpu.VMEM((FOX_HEADS, tq, FOX_DH + LANES), F32)]),
        out_shape=jax.ShapeDtypeStruct((t, w), BF16),
        compiler_params=_params(("parallel", "arbitrary")),
        name="attn_prompt",
    )(qi, kj, q, k, v, nbias)


def _attn_sample_kernel(q_ref, kc_ref, vc_ref, kn_ref, vn_ref, nb_ref, o_ref):
    tq = q_ref.shape[0]
    past = kc_ref.shape[2] // FOX_HEADS
    nb = nb_ref[0]
    heads = range(FOX_HEADS)
    sls = [slice(h * FOX_DH, (h + 1) * FOX_DH) for h in heads]
    nt = (((1,), (1,)), ((), ()))
    causal = (lax.broadcasted_iota(jnp.int32, (tq, tq), 0)
              >= lax.broadcasted_iota(jnp.int32, (tq, tq), 1))
    kc = [kc_ref[0, 0, pl.ds(h, past, stride=FOX_HEADS), :].astype(BF16) for h in heads]
    s_c = [lax.dot_general(q_ref[:, sl], x, nt, preferred_element_type=F32) + nb[h:h + 1, :past]
           for h, sl, x in zip(heads, sls, kc)]
    s_n = [jnp.where(causal,
                     lax.dot_general(q_ref[:, sl], kn_ref[:, sl], nt,
                                     preferred_element_type=F32) + nb[h:h + 1, past:], NEG_BIG)
           for h, sl in zip(heads, sls)]
    m = [jnp.maximum(jnp.max(a, axis=-1, keepdims=True), jnp.max(b, axis=-1, keepdims=True))
         for a, b in zip(s_c, s_n)]
    p_c = [jnp.exp2(a - mm) for a, mm in zip(s_c, m)]
    p_n = [jnp.exp2(b - mm) for b, mm in zip(s_n, m)]
    l = [jnp.sum(a, axis=-1, keepdims=True) + jnp.sum(b, axis=-1, keepdims=True)
         for a, b in zip(p_c, p_n)]
    vc = [vc_ref[0, 0, pl.ds(h, past, stride=FOX_HEADS), :].astype(BF16) for h in heads]
    o = [jnp.dot(a.astype(BF16), x, preferred_element_type=F32)
         + jnp.dot(b.astype(BF16), vn_ref[:, sl], preferred_element_type=F32)
         for a, b, x, sl in zip(p_c, p_n, vc, sls)]
    for h, sl in zip(heads, sls):
        o_ref[:, sl] = (o[h] / l[h]).astype(o_ref.dtype)


def _attn_sample(q, k_new, v_new, k_cache, v_cache, layer, nbias, nseq):
    t, w = q.shape
    tq = t // nseq
    past = k_cache.shape[2] // FOX_HEADS
    row = pl.BlockSpec((tq, w), lambda b: (b, 0))
    cache = pl.BlockSpec((1, 1) + k_cache.shape[2:], lambda b: (layer, b, 0, 0))
    return pl.pallas_call(
        _attn_sample_kernel,
        grid=(nseq,),
        in_specs=[row, cache, cache, row, row,
                  pl.BlockSpec((1, FOX_HEADS, past + tq), lambda b: (b, 0, 0))],
        out_specs=row,
        out_shape=jax.ShapeDtypeStruct((t, w), BF16),
        compiler_params=_params(("parallel",)),
        name="attn_sample",
    )(q, k_cache, v_cache, k_new, v_new, nbias)


def _pad_lanes(w):
    return jnp.pad(w, ((0, 0), (0, LANES - w.shape[1])))


def _row(v, width=None):
    v = v.reshape(1, -1).astype(F32)
    if width is not None:
        v = jnp.pad(v, ((0, 0), (0, width - v.shape[1])))
    return v


def _prep_weights(norm_mix_g, norm_ffn_g, w_in_even, gdn_conv_w, gdn_a_log, gdn_dt_bias,
                  gdn_norm_g, sconv_w, w_out_even, w_in_odd, fox_b_f, fox_q_norm_g,
                  fox_k_norm_g, w_out_odd, ffn_w_gate, ffn_w_up, ffn_w_down):
    depth = norm_mix_g.shape[0]
    nqkv = gdn_conv_w.shape[2]
    nz = GDN_HEADS * GDN_DV
    nsc = sconv_w.shape[2]
    o1, o2 = nqkv, nqkv + nz
    o4 = o2 + 2 * GDN_HEADS
    o5, o6 = o4 + nsc, o4 + 2 * nsc
    fw = FOX_HEADS * FOX_DH
    shared = dict(g_ffn=norm_ffn_g.astype(F32).reshape(depth, 1, -1),
                  wg=ffn_w_gate.astype(BF16), wu=ffn_w_up.astype(BF16),
                  wd=ffn_w_down.astype(BF16),
                  wout_even=w_out_even.astype(BF16), wout_odd=w_out_odd.astype(BF16))
    layers = []
    for layer in range(depth):
        i = layer // 2
        p = dict(g_mix=_row(norm_mix_g[layer]))
        if layer % 2 == 0:
            w = w_in_even[i]
            p.update(wqkv=w[:, :o1].astype(BF16), wz=w[:, o1:o2].astype(BF16),
                     wab=_pad_lanes(w[:, o2:o4]).astype(BF16),
                     wgb=w[:, o4:o5].astype(BF16), wgc=w[:, o5:o6].astype(BF16),
                     whin=w[:, o6:].astype(BF16),
                     convw=gdn_conv_w[i].astype(F32), alog=_row(gdn_a_log[i], LANES),
                     dtb=_row(gdn_dt_bias[i], LANES), ng=_row(gdn_norm_g[i]),
                     scw=sconv_w[i].astype(F32))
        else:
            w = w_in_odd[i]
            p.update(wq=w[:, :fw].astype(BF16), wk=w[:, fw:2 * fw].astype(BF16),
                     wv=w[:, 2 * fw:3 * fw].astype(BF16),
                     wf=_pad_lanes(w[:, 3 * fw:]).astype(BF16),
                     bf=_row(fox_b_f[i], LANES), qg=_row(fox_q_norm_g[i]),
                     kg=_row(fox_k_norm_g[i]))
        layers.append(p)
    return layers, shared


class _Trunk:
    def __init__(self, x, gdn_conv, gdn_s, sconv, fox_k, fox_v, fox_logf):
        self.nseq, self.seq, self.d = x.shape
        self.x = x.reshape(self.nseq * self.seq, self.d)
        self.has_past = fox_k is not None
        if self.has_past:
            fox_k = fox_k.reshape(fox_k.shape[:2] + (-1, FOX_DH))
            fox_v = fox_v.reshape(fox_v.shape[:2] + (-1, FOX_DH))
        self.gdn_conv, self.gdn_s, self.sconv = gdn_conv, gdn_s, sconv
        self.fox_k, self.fox_v, self.fox_logf = fox_k, fox_v, fox_logf
        self.new_lf, self.new_conv, self.new_sc = [], [], []
        self.k_all = self.v_all = self.s_all = None

    def mixer(self, layer, p, nlayers):
        i = layer // 2
        nseq, seq = self.nseq, self.seq
        if layer % 2 == 0:
            if self.has_past:
                conv0, s_in, li, sc0 = self.gdn_conv[i], self.gdn_s, i, self.sconv[i]
            else:
                conv0 = jnp.zeros((nseq, GDN_CONV - 1, p["wqkv"].shape[1]), F32)
                s_in, li = jnp.zeros((1, nseq, GDN_HEADS, GDN_DK, GDN_DV), F32), 0
                sc0 = jnp.zeros((nseq, SC_CONV - 1, p["wgb"].shape[1]), F32)
            y, z, ab, mix_b, conv_fin, sc_fin = _in_even(
                self.x, p["g_mix"], p["wqkv"], p["wz"], p["wab"], p["wgb"], p["wgc"], p["whin"],
                conv0, sc0, p["convw"], p["scw"], nseq)
            mix_a, self.s_all = _gdn(y, z, ab, s_in, li, p["alog"], p["dtb"], p["ng"], i,
                                     (nlayers + 1) // 2, self.s_all)
            self.new_conv.append(conv_fin)
            self.new_sc.append(sc_fin)
            return mix_a, mix_b
        q, self.k_all, kb, self.v_all, vb, lf = _in_odd(
            self.x, p["g_mix"], p["wq"], p["wk"], p["wv"], p["wf"], p["bf"], p["qg"], p["kg"], i,
            nlayers // 2, self.k_all, self.v_all)
        lf = lf.reshape(nseq, seq, FOX_HEADS)
        self.new_lf.append(lf)
        if self.has_past:
            lf_all = jnp.concatenate([self.fox_logf[i].astype(F32), lf], axis=1)
            nbias = _key_bias(lf_all.transpose(0, 2, 1), ATTN_TILE)
            return _attn_sample(q, kb, vb, self.fox_k, self.fox_v, i, nbias, nseq), None
        nbias = _key_bias(lf.transpose(0, 2, 1), ATTN_TILE)
        return _attn_prompt(q, kb, vb, nbias, nseq), None

    def outputs(self, nlayers):
        kv_shape = (nlayers // 2, self.nseq, self.seq, FOX_HEADS, FOX_DH)
        return (self.x.reshape(self.nseq, self.seq, self.d), self.k_all.reshape(kv_shape),
                self.v_all.reshape(kv_shape), jnp.stack(self.new_lf), self.s_all,
                jnp.stack(self.new_conv), jnp.stack(self.new_sc))


def kernel(x_prompt, x_sample, cache_fox_k, cache_fox_v, cache_fox_logf, state_gdn_S,
           state_gdn_conv, state_sconv, norm_mix_g, norm_ffn_g, w_in_even, gdn_conv_w,
           gdn_a_log, gdn_dt_bias, gdn_norm_g, sconv_w, w_out_even, w_in_odd, fox_b_f,
           fox_q_norm_g, fox_k_norm_g, w_out_odd, ffn_w_gate, ffn_w_up, ffn_w_down):
    layers, shared = _prep_weights(
        norm_mix_g, norm_ffn_g, w_in_even, gdn_conv_w, gdn_a_log, gdn_dt_bias, gdn_norm_g,
        sconv_w, w_out_even, w_in_odd, fox_b_f, fox_q_norm_g, fox_k_norm_g, w_out_odd,
        ffn_w_gate, ffn_w_up, ffn_w_down)
    trunks = (_Trunk(x_prompt, None, None, None, None, None, None),
              _Trunk(x_sample, state_gdn_conv, state_gdn_S, state_sconv, cache_fox_k, cache_fox_v,
                     cache_fox_logf))
    nlayers = len(layers)
    for layer, p in enumerate(layers):
        mixes = tuple(t.mixer(layer, p, nlayers) for t in trunks)
        wout = shared["wout_even"] if layer % 2 == 0 else shared["wout_odd"]
        xs = _out_ffn(tuple(t.x for t in trunks), mixes, wout, layer // 2, shared["g_ffn"],
                      shared["wg"], shared["wu"], shared["wd"], layer)
        for t, x in zip(trunks, xs):
            t.x = x
    (y_prompt, p_fox_k, p_fox_v, p_fox_logf, p_gdn_s, p_gdn_conv, p_sconv) = trunks[0].outputs(nlayers)
    (y_sample, s_fox_k, s_fox_v, s_fox_logf, s_gdn_s, s_gdn_conv, s_sconv) = trunks[1].outputs(nlayers)
    return (y_prompt, y_sample, p_fox_k, p_fox_v, p_fox_logf, p_gdn_s, p_gdn_conv, p_sconv,
            s_fox_k, s_fox_v, s_fox_logf, s_gdn_s, s_gdn_conv, s_sconv)
```

```python
import functools

import jax
import jax.numpy as jnp
from jax import lax
from jax.experimental import pallas as pl
from jax.experimental.pallas import tpu as pltpu

F32 = jnp.float32
BF16 = jnp.bfloat16

EPS = 1e-6
CHUNK = 64
GDN_HEADS = 4
GDN_DK = 128
GDN_DV = 128
GDN_CONV = 4
SC_CONV = 3
FOX_HEADS = 8
FOX_DH = 128
LANES = 128
SUBLANES = 8
INV_BLOCK = 16
LOG2E = 1.4426950408889634
NEG_BIG = -1e30
VMEM_LIMIT_BYTES = 56 * 1024 * 1024

TOKEN_TILE = 512
FFN_TOKEN_TILE = 512
MXU_COLS = 256
ATTN_TILE = 256
ATTN_Q_BLOCKS = 4
GDN_BLOCKS_PER_STEP = 8


def _params(semantics):
    return pltpu.CompilerParams(dimension_semantics=semantics,
                                vmem_limit_bytes=VMEM_LIMIT_BYTES)


def _mm(a, b):
    return jnp.dot(a.astype(BF16), b.astype(BF16), preferred_element_type=F32)


def _mm_nt(a, b):
    return lax.dot_general(a.astype(BF16), b.astype(BF16), (((1,), (1,)), ((), ())),
                           preferred_element_type=F32)


def _split3(x):
    hi = x.astype(BF16)
    r1 = x - hi.astype(F32)
    mid = r1.astype(BF16)
    lo = (r1 - mid.astype(F32)).astype(BF16)
    return hi, mid, lo


def _rms_norm(x, g):
    return x * lax.rsqrt(jnp.mean(x * x, axis=-1, keepdims=True) + EPS) * g


def _softplus(x):
    return jnp.maximum(x, 0.0) + jnp.log1p(jnp.exp(-jnp.abs(x)))


def _sigmoid(x):
    return 1.0 / (1.0 + jnp.exp(-x))


def _silu(x):
    return x * _sigmoid(x)


def _in_even_kernel(ns, rows, x_ref, g_ref, wqkv_ref, wz_ref, wab_ref, wgb_ref, wgc_ref,
                    whin_ref, conv0_ref, sc0_ref, convw_ref, scw_ref,
                    y_ref, z_ref, ab_ref, ob_ref, conv_ref, sc_ref, xslab, uslab):
    top = SUBLANES
    kc, ks = GDN_CONV - 1, SC_CONV - 1

    @pl.when(pl.program_id(1) == 0)
    def _():
        conv_ref[...] = conv0_ref[...]
        sc_ref[...] = sc0_ref[...]

    x = x_ref[...].reshape(ns * rows, x_ref.shape[2])
    h = _rms_norm(x, g_ref[...]).astype(BF16)
    cw = convw_ref[...]
    sw = scw_ref[...]

    def causal_conv(t, c0, slab, tail_ref, w, taps, s):
        for j in range(t.shape[1] // LANES):
            cs = slice(c0 + j * LANES, c0 + (j + 1) * LANES)
            k = cs.start // LANES
            ts = t[s * rows:(s + 1) * rows, j * LANES:(j + 1) * LANES]
            slab[s, k, top - taps:top, :] = tail_ref[s, :, cs]
            slab[s, k, top:top + rows, :] = ts
            tail_ref[s, :, cs] = ts[rows - taps:rows]
            acc = ts * w[taps:taps + 1, cs]
            for d in range(1, taps + 1):
                acc = acc + slab[s, k, top - d:top - d + rows, :] * w[taps - d:taps - d + 1, cs]
            yield cs, j, acc

    def dot(w_ref, cols):
        return jnp.dot(h, w_ref[:, cols], preferred_element_type=F32)

    jobs = []
    for c in range(0, wqkv_ref.shape[1], MXU_COLS):
        def conv_silu(t, c=c):
            for s in range(ns):
                for cs, _, acc in causal_conv(t, c, xslab, conv_ref, cw, kc, s):
                    y_ref[s, :, cs] = _silu(acc)
        jobs.append((lambda c=c: dot(wqkv_ref, slice(c, c + MXU_COLS)), conv_silu))
    for c in range(0, wgb_ref.shape[1], MXU_COLS):
        def gated_conv(t, c=c):
            gate_c, h_in, gate_b = t
            u = gate_c * h_in
            for s in range(ns):
                for cs, j, acc in causal_conv(u, c, uslab, sc_ref, sw, ks, s):
                    gb = gate_b[s * rows:(s + 1) * rows, j * LANES:(j + 1) * LANES]
                    ob_ref[s, :, cs] = (gb * acc).astype(ob_ref.dtype)
        cols = slice(c, c + MXU_COLS)
        jobs.append((lambda cols=cols: (dot(wgc_ref, cols), dot(whin_ref, cols), dot(wgb_ref, cols)),
                     gated_conv))

    def store_z(t):
        z_ref[...] = t.astype(BF16).reshape(z_ref.shape)

    def store_ab(t):
        ab_ref[...] = t.reshape(ab_ref.shape)
    jobs.append((lambda: jnp.dot(h, wz_ref[...], preferred_element_type=F32), store_z))
    jobs.append((lambda: jnp.dot(h, wab_ref[...], preferred_element_type=F32), store_ab))

    nxt = jobs[0][0]()
    for n, (_, post) in enumerate(jobs):
        cur = nxt
        if n + 1 < len(jobs):
            nxt = jobs[n + 1][0]()
        post(cur)


def _stream_blocking(nseq, seq, rows_per_step):
    rows = min(seq, rows_per_step)
    ns = max(1, min(nseq, rows_per_step // rows))
    return ns, rows


def _in_even(x, g, wqkv, wz, wab, wgb, wgc, whin, conv0, sc0, convw, scw, nseq):
    t, d = x.shape
    seq = t // nseq
    ns, rows = _stream_blocking(nseq, seq, TOKEN_TILE)
    nqkv, nz, nsc = wqkv.shape[1], wz.shape[1], wgb.shape[1]
    row = lambda n: pl.BlockSpec((ns, rows, n), lambda b, c: (b, c, 0))
    seq3 = lambda a: pl.BlockSpec((ns,) + a.shape[1:], lambda b, c: (b, 0, 0))
    full = lambda a: pl.BlockSpec(a.shape, lambda b, c: (0, 0))
    y, z, ab, ob, conv_fin, sc_fin = pl.pallas_call(
        functools.partial(_in_even_kernel, ns, rows),
        grid=(nseq // ns, seq // rows),
        in_specs=[row(d), full(g), full(wqkv), full(wz), full(wab), full(wgb), full(wgc),
                  full(whin), seq3(conv0), seq3(sc0), full(convw), full(scw)],
        out_specs=[row(nqkv), row(nz), row(LANES), row(nsc), seq3(conv0), seq3(sc0)],
        out_shape=[jax.ShapeDtypeStruct((nseq, seq, nqkv), F32),
                   jax.ShapeDtypeStruct((nseq, seq, nz), BF16),
                   jax.ShapeDtypeStruct((nseq, seq, LANES), F32),
                   jax.ShapeDtypeStruct((nseq, seq, nsc), BF16),
                   jax.ShapeDtypeStruct(conv0.shape, F32),
                   jax.ShapeDtypeStruct(sc0.shape, F32)],
        scratch_shapes=[pltpu.VMEM((ns, nqkv // LANES, SUBLANES + rows, LANES), F32),
                        pltpu.VMEM((ns, nsc // LANES, SUBLANES + rows, LANES), F32)],
        compiler_params=_params(("parallel", "arbitrary")),
        name="in_even",
    )(x.reshape(nseq, seq, d), g, wqkv, wz, wab, wgb, wgc, whin, conv0, sc0, convw, scw)
    return y, z, ab, ob.reshape(t, nsc), conv_fin, sc_fin


def _unit_lower_inverse_many(a_list):
    c = a_list[0].shape[0]
    ii = lax.broadcasted_iota(jnp.int32, (c, c), 0)
    jj = lax.broadcasted_iota(jnp.int32, (c, c), 1)
    eye = (ii == jj).astype(F32)
    same_block = (ii // INV_BLOCK) == (jj // INV_BLOCK)
    a0 = [jnp.where(same_block, a, 0.0) for a in a_list]
    a1 = [a - z for a, z in zip(a_list, a0)]
    d_inv = [eye - z for z in a0]
    power = a0
    for _ in range(INV_BLOCK.bit_length() - 2):
        power = [_mm(p, p) for p in power]
        d_inv = [d + _mm(d, p) for d, p in zip(d_inv, power)]
    n = [_mm(d, r) for d, r in zip(d_inv, a1)]
    m = [eye - x for x in n]
    power = n
    for _ in range(max((c // INV_BLOCK).bit_length() - 2, 0)):
        power = [_mm(p, p) for p in power]
        m = [x + _mm(x, p) for x, p in zip(m, power)]
    return [_mm(x, d) for x, d in zip(m, d_inv)]


def _gdn_kernel(ns, nc, y_ref, z_ref, ab_ref, s0_ref, alog_ref, dtb_ref, ng_ref, *rest):
    out_ref, s_ref = rest[-2:]
    ch = CHUNK
    nk = GDN_HEADS * GDN_DK

    @pl.when(pl.program_id(1) == 0)
    def _():
        s_ref[...] = s0_ref[...]

    alog, dtb, ng = alog_ref[...], dtb_ref[...], ng_ref[...]
    ii = lax.broadcasted_iota(jnp.int32, (ch, ch), 0)
    jj = lax.broadcasted_iota(jnp.int32, (ch, ch), 1)
    incl = ii >= jj
    strict = ii > jj
    tri = incl.astype(BF16)

    blocks = [(s, k) for s in range(ns) for k in range(nc)]
    chains = [(b, h) for b in range(len(blocks)) for h in range(GDN_HEADS)]

    ys, gcums, gcum_ts, betas = [], [], [], []
    for s, k in blocks:
        ys.append(y_ref[s, k * ch:(k + 1) * ch, :])
        ab = ab_ref[s, k * ch:(k + 1) * ch, :]
        g = -jnp.exp(alog) * _softplus(ab + dtb)
        betas.append(_sigmoid(ab))
        g_hi, g_mid, g_lo = _split3(g)
        gcum = (jnp.dot(tri, g_hi, preferred_element_type=F32)
                + jnp.dot(tri, g_mid, preferred_element_type=F32)
                + jnp.dot(tri, g_lo, preferred_element_type=F32))
        gcums.append(gcum)
        gcum_ts.append(gcum.T)

    def l2n(x):
        return x * lax.rsqrt(jnp.sum(x * x, axis=-1, keepdims=True) + EPS)

    q = [l2n(ys[b][:, h * GDN_DK:(h + 1) * GDN_DK]) * (GDN_DK ** -0.5) for b, h in chains]
    kk = [l2n(ys[b][:, nk + h * GDN_DK:nk + (h + 1) * GDN_DK]) for b, h in chains]
    v = [ys[b][:, 2 * nk + h * GDN_DV:2 * nk + (h + 1) * GDN_DV] for b, h in chains]
    gc = [gcums[b][:, h:h + 1] for b, h in chains]
    gr = [gcum_ts[b][h:h + 1, :] for b, h in chains]
    g_last = [gcums[b][ch - 1:ch, h:h + 1] for b, h in chains]
    bh = [betas[b][:, GDN_HEADS + h:GDN_HEADS + h + 1] for b, h in chains]
    decay = [jnp.exp(jnp.where(incl, c_ - r_, -jnp.inf)) for c_, r_ in zip(gc, gr)]
    kb = [x * b_ for x, b_ in zip(kk, bh)]
    a = [_mm_nt(x, y_) * jnp.where(strict, d, 0.0) for x, y_, d in zip(kb, kk, decay)]
    t_inv = _unit_lower_inverse_many(a)
    eg = [jnp.exp(c_) for c_ in gc]
    rhs = [jnp.concatenate([v_ * b_, x * e], axis=-1) for v_, b_, x, e in zip(v, bh, kb, eg)]
    sol = [_mm(t, r).astype(BF16) for t, r in zip(t_inv, rhs)]
    p = [(_mm_nt(x, y_) * d).astype(BF16) for x, y_, d in zip(q, kk, decay)]
    kd = [(x * jnp.exp(l - c_)).astype(BF16) for x, l, c_ in zip(kk, g_last, gc)]
    ksol = [lax.dot_general(x, y_, (((0,), (0,)), ((), ())), preferred_element_type=F32)
            for x, y_ in zip(kd, sol)]
    psol = [jnp.dot(x, y_, preferred_element_type=F32) for x, y_ in zip(p, sol)]
    lhs = [jnp.concatenate([-ks_[:, GDN_DV:], x * e - ps_[:, GDN_DV:]], axis=0).astype(BF16)
           for ks_, x, e, ps_ in zip(ksol, q, eg, psol)]
    sdecay = [jnp.exp(l) for l in g_last]

    state = [[s_ref[s, h] for h in range(GDN_HEADS)] for s in range(ns)]
    for k in range(nc):
        idx = [(s, h, (s * nc + k) * GDN_HEADS + h) for s in range(ns) for h in range(GDN_HEADS)]
        r = [jnp.dot(lhs[c], state[s][h].astype(BF16), preferred_element_type=F32)
             for s, h, c in idx]
        for (s, h, c), r_ in zip(idx, r):
            state[s][h] = state[s][h] * sdecay[c] + r_[:GDN_DK] + ksol[c][:, :GDN_DV]
        for (s, h, c), r_ in zip(idx, r):
            o = r_[GDN_DK:] + psol[c][:, :GDN_DV]
            zh = z_ref[s, k * ch:(k + 1) * ch, h * GDN_DV:(h + 1) * GDN_DV].astype(F32)
            o = _rms_norm(o, ng) * _silu(zh)
            out_ref[s, k * ch:(k + 1) * ch, h * GDN_DV:(h + 1) * GDN_DV] = o.astype(out_ref.dtype)
    for s in range(ns):
        for h in range(GDN_HEADS):
            s_ref[s, h] = state[s][h]


def _gdn(y, z, ab, s_in, li, alog, dtb, ng, slot, nslots, s_all):
    nseq, seq, nqkv = y.shape
    nz = z.shape[2]
    nchunk = seq // CHUNK
    ns = min(nseq, GDN_BLOCKS_PER_STEP)
    nc = max(1, min(nchunk, GDN_BLOCKS_PER_STEP // ns))
    rows = nc * CHUNK
    row = lambda n: pl.BlockSpec((ns, rows, n), lambda b, c: (b, c, 0))
    state = lambda layer: pl.BlockSpec((None, ns) + s_in.shape[2:],
                                       lambda b, c: (layer, b, 0, 0, 0))
    full = lambda a: pl.BlockSpec(a.shape, lambda b, c: (0, 0))
    args = [y, z, ab, s_in, alog, dtb, ng]
    in_specs = [row(nqkv), row(nz), row(LANES), state(li), full(alog), full(dtb), full(ng)]
    aliases = {}
    if s_all is not None:
        aliases = {len(args): 1}
        args.append(s_all)
        in_specs.append(pl.BlockSpec(memory_space=pl.ANY))
    o, s_all = pl.pallas_call(
        functools.partial(_gdn_kernel, ns, nc),
        grid=(nseq // ns, nchunk // nc),
        in_specs=in_specs,
        out_specs=[row(nz), state(slot)],
        out_shape=[jax.ShapeDtypeStruct((nseq, seq, nz), BF16),
                   jax.ShapeDtypeStruct((nslots,) + s_in.shape[1:], F32)],
        input_output_aliases=aliases,
        compiler_params=_params(("parallel", "arbitrary")),
        name="gdn",
    )(*args)
    return o.reshape(nseq * seq, nz), s_all


def _out_ffn_kernel(nfirst, *refs):
    (x0, a0, b0, x1, a1, b1, wa_ref, wb_ref, g_ref, wg_ref, wu_ref, wd_ref, o0, o1, act) = refs

    def tile(x_ref, ma_ref, mb_ref, o_ref):
        xn = (x_ref[...] + jnp.dot(ma_ref[...], wa_ref[0], preferred_element_type=F32)
              + jnp.dot(mb_ref[...], wb_ref[0], preferred_element_type=F32))
        h = _rms_norm(xn, g_ref[0]).astype(BF16)
        for c in range(0, wg_ref.shape[2], MXU_COLS):
            gate = jnp.dot(h, wg_ref[0, :, c:c + MXU_COLS], preferred_element_type=F32)
            up = jnp.dot(h, wu_ref[0, :, c:c + MXU_COLS], preferred_element_type=F32)
            act[:, c:c + MXU_COLS] = (_silu(gate) * up).astype(BF16)
        o_ref[...] = xn + jnp.dot(act[...], wd_ref[0], preferred_element_type=F32)

    @pl.when(pl.program_id(0) < nfirst)
    def _():
        tile(x0, a0, b0, o0)

    @pl.when(pl.program_id(0) >= nfirst)
    def _():
        tile(x1, a1, b1, o1)


def _out_ffn(xs, mixes, wout, iw, g, wg, wu, wd, layer):
    d = xs[0].shape[1]
    tm = min(FFN_TOKEN_TILE, xs[0].shape[0], xs[1].shape[0])
    n0, n1 = xs[0].shape[0] // tm, xs[1].shape[0] // tm
    dff = wg.shape[2]
    half = wout.shape[1] // 2
    imaps = [lambda i: jnp.minimum(i, n0 - 1), lambda i: jnp.maximum(i - n0, 0)]
    once = lambda shape, imap: pl.BlockSpec(shape, imap, pipeline_mode=pl.Buffered(1))
    args, in_specs, out_specs, out_shape = [], [], [], []
    for x, (ma, mb), im in zip(xs, mixes, imaps):
        col_b = 0
        if mb is None:
            mb, col_b = ma, 1
        args += [x, ma, mb]
        in_specs += [pl.BlockSpec((tm, d), lambda i, im=im: (im(i), 0)),
                     pl.BlockSpec((tm, half), lambda i, im=im: (im(i), 0)),
                     pl.BlockSpec((tm, half), lambda i, im=im, c=col_b: (im(i), c))]
        out_specs.append(pl.BlockSpec((tm, d), lambda i, im=im: (im(i), 0)))
        out_shape.append(jax.ShapeDtypeStruct(x.shape, F32))
    args += [wout, wout, g, wg, wu, wd]
    in_specs += [once((1, half, d), lambda i: (iw, 0, 0)),
                 once((1, half, d), lambda i: (iw, 1, 0)),
                 once((1, 1, d), lambda i: (layer, 0, 0)),
                 once((1, d, dff), lambda i: (layer, 0, 0)),
                 once((1, d, dff), lambda i: (layer, 0, 0)),
                 once((1, dff, d), lambda i: (layer, 0, 0))]
    return pl.pallas_call(
        functools.partial(_out_ffn_kernel, n0),
        grid=(n0 + n1,),
        in_specs=in_specs,
        out_specs=out_specs,
        out_shape=out_shape,
        scratch_shapes=[pltpu.VMEM((tm, dff), BF16)],
        compiler_params=_params(("arbitrary",)),
        name="out_ffn",
    )(*args)


def _in_odd_kernel(x_ref, g_ref, wq_ref, wk_ref, wv_ref, wf_ref, bf_ref, qg_ref, kg_ref, *rest):
    q_ref, k_ref, kb_ref, v_ref, vb_ref, lf_ref = rest[-6:]
    tm = x_ref.shape[0]
    h = _rms_norm(x_ref[...], g_ref[...]).astype(BF16)
    q = jnp.dot(h, wq_ref[...], preferred_element_type=F32)
    k = jnp.dot(h, wk_ref[...], preferred_element_type=F32)
    v = jnp.dot(h, wv_ref[...], preferred_element_type=F32)
    qg, kg = qg_ref[...], kg_ref[...]
    qscale = (FOX_DH ** -0.5) * LOG2E
    for hd in range(FOX_HEADS):
        sl = slice(hd * FOX_DH, (hd + 1) * FOX_DH)
        qn = _rms_norm(q[:, sl], qg)
        kn = _rms_norm(k[:, sl], kg)
        q_ref[:, sl] = (qn * qscale).astype(BF16)
        k_ref[0, pl.ds(hd, tm, stride=FOX_HEADS), :] = kn
        kb_ref[:, sl] = kn.astype(BF16)
        v_ref[0, pl.ds(hd, tm, stride=FOX_HEADS), :] = v[:, sl]
    vb_ref[...] = v.astype(BF16)
    f = jnp.dot(h, wf_ref[...], preferred_element_type=F32) + bf_ref[...]
    lf_ref[...] = (-_softplus(-f))[:, :FOX_HEADS]


def _in_odd(x, g, wq, wk, wv, wf, bf, qg, kg, slot, nslots, k_all, v_all):
    t, d = x.shape
    tm = min(TOKEN_TILE, t)
    w = wq.shape[1]
    row = lambda n: pl.BlockSpec((tm, n), lambda i: (i, 0))
    full = lambda a: pl.BlockSpec(a.shape, lambda i: (0, 0))
    stacked = pl.BlockSpec((1, tm * FOX_HEADS, FOX_DH), lambda i: (slot, i, 0))
    stacked_shape = jax.ShapeDtypeStruct((nslots, t * FOX_HEADS, FOX_DH), F32)
    args = [x, g, wq, wk, wv, wf, bf, qg, kg]
    in_specs = [row(d), full(g), full(wq), full(wk), full(wv), full(wf), full(bf),
                full(qg), full(kg)]
    aliases = {}
    if k_all is not None:
        aliases = {len(args): 1, len(args) + 1: 3}
        args += [k_all, v_all]
        in_specs += [pl.BlockSpec(memory_space=pl.ANY)] * 2
    return pl.pallas_call(
        _in_odd_kernel,
        grid=(t // tm,),
        in_specs=in_specs,
        out_specs=[row(w), stacked, row(w), stacked, row(w), row(FOX_HEADS)],
        out_shape=[jax.ShapeDtypeStruct((t, w), BF16), stacked_shape,
                   jax.ShapeDtypeStruct((t, w), BF16), stacked_shape,
                   jax.ShapeDtypeStruct((t, w), BF16),
                   jax.ShapeDtypeStruct((t, FOX_HEADS), F32)],
        input_output_aliases=aliases,
        compiler_params=_params(("parallel",)),
        name="in_odd",
    )(*args)


def _key_bias_kernel(lf_ref, o_ref, carry):
    @pl.when(pl.program_id(0) == 0)
    def _():
        carry[...] = jnp.zeros_like(carry)

    lf = lf_ref[...]
    tt = lf.shape[1]
    upper = (lax.broadcasted_iota(jnp.int32, (tt, tt), 0)
             <= lax.broadcasted_iota(jnp.int32, (tt, tt), 1)).astype(BF16)
    hi, mid, lo = _split3(lf)
    csum = (jnp.dot(hi, upper, preferred_element_type=F32)
            + jnp.dot(mid, upper, preferred_element_type=F32)
            + jnp.dot(lo, upper, preferred_element_type=F32)) + carry[:, 0:1]
    o_ref[...] = csum * (-LOG2E)
    carry[...] = jnp.broadcast_to(csum[:, tt - 1:tt], carry.shape)


def _key_bias(lf_t, tile):
    b, h, s = lf_t.shape
    tt = tile if s % tile == 0 else s
    return pl.pallas_call(
        _key_bias_kernel,
        grid=(s // tt,),
        in_specs=[pl.BlockSpec((b * h, tt), lambda j: (0, j))],
        out_specs=pl.BlockSpec((b * h, tt), lambda j: (0, j)),
        out_shape=jax.ShapeDtypeStruct((b * h, s), F32),
        scratch_shapes=[pltpu.VMEM((b * h, LANES), F32)],
        compiler_params=_params(("arbitrary",)),
        name="key_bias",
    )(lf_t.reshape(b * h, s)).reshape(b, h, s)


def _attn_prompt_kernel(ratio, qi_ref, kj_ref, q_ref, k_ref, v_ref, nb_ref, o_ref, m_ref, acc_ref):
    i, j = qi_ref[pl.program_id(1)], kj_ref[pl.program_id(1)]
    tk = k_ref.shape[0] // ratio
    heads = range(FOX_HEADS)
    sls = [slice(h * FOX_DH, (h + 1) * FOX_DH) for h in heads]

    @pl.when(j == 0)
    def _():
        m_ref[...] = jnp.full_like(m_ref, NEG_BIG)
        acc_ref[...] = jnp.zeros_like(acc_ref)

    def step(sub, row0, masked):
        nr = q_ref.shape[0] - row0
        keys = slice(sub * tk, (sub + 1) * tk)
        nb = nb_ref[0, :, keys]
        ones = jnp.ones((tk, LANES), BF16)
        reps = tk // LANES
        if masked:
            causal = (lax.broadcasted_iota(jnp.int32, (nr, tk), 0)
                      >= lax.broadcasted_iota(jnp.int32, (nr, tk), 1))
        rows = slice(row0, q_ref.shape[0])
        for h, sl in zip(heads, sls):
            s = lax.dot_general(q_ref[rows, sl], k_ref[keys, sl], (((1,), (1,)), ((), ())),
                                preferred_element_type=F32) + nb[h:h + 1, :]
            if masked:
                s = jnp.where(causal, s, NEG_BIG)
            m_prev = m_ref[h, rows, :]
            m_new = jnp.maximum(m_prev, jnp.max(s, axis=-1, keepdims=True))
            alpha = jnp.exp2(m_prev - m_new)
            p = jnp.exp2(s - jnp.concatenate([m_new] * reps, axis=-1)).astype(BF16)
            pv = jnp.dot(p, jnp.concatenate([v_ref[keys, sl], ones], axis=-1),
                         preferred_element_type=F32)
            acc_ref[h, rows, :] = jnp.concatenate([alpha, alpha], axis=-1) * acc_ref[h, rows, :] + pv
            m_ref[h, rows, :] = m_new

    @pl.when(j < i)
    def _():
        for sub in range(ratio):
            step(sub, 0, False)

    @pl.when(j == i)
    def _():
        for sub in range(ratio):
            step(sub, sub * tk, True)
        for h, sl in zip(heads, sls):
            acc = acc_ref[h]
            o_ref[:, sl] = (acc[:, :FOX_DH] / acc[:, FOX_DH:]).astype(o_ref.dtype)


def _attn_prompt(q, k, v, nbias, nseq):
    t, w = q.shape
    s = t // nseq
    tk = min(ATTN_TILE, s)
    ratio = ATTN_Q_BLOCKS if s % (ATTN_Q_BLOCKS * tk) == 0 else 1
    tq = ratio * tk
    nblk = s // tq
    pairs = [(i, j) for i in range(nblk) for j in range(i + 1)]
    qi = jnp.asarray([p[0] for p in pairs], jnp.int32)
    kj = jnp.asarray([p[1] for p in pairs], jnp.int32)
    q_spec = pl.BlockSpec((tq, w), lambda b, n, qi, kj: (b * nblk + qi[n], 0))
    kv_spec = pl.BlockSpec((tq, w), lambda b, n, qi, kj: (b * nblk + kj[n], 0))
    return pl.pallas_call(
        functools.partial(_attn_prompt_kernel, ratio),
        grid_spec=pltpu.PrefetchScalarGridSpec(
            num_scalar_prefetch=2,
            grid=(nseq, len(pairs)),
            in_specs=[q_spec, kv_spec, kv_spec,
                      pl.BlockSpec((1, FOX_HEADS, tq), lambda b, n, qi, kj: (b, 0, kj[n]))],
            out_specs=q_spec,
            scratch_shapes=[pltpu.VMEM((FOX_HEADS, tq, LANES), F32),
                            pltpu.VMEM((FOX_HEADS, tq, FOX_DH + LANES), F32)]),
        out_shape=jax.ShapeDtypeStruct((t, w), BF16),
        compiler_params=_params(("parallel", "arbitrary")),
        name="attn_prompt",
    )(qi, kj, q, k, v, nbias)


def _attn_sample_kernel(q_ref, kc_ref, vc_ref, kn_ref, vn_ref, nb_ref, o_ref):
    tq = q_ref.shape[0]
    past = kc_ref.shape[2] // FOX_HEADS
    nb = nb_ref[0]
    heads = range(FOX_HEADS)
    sls = [slice(h * FOX_DH, (h + 1) * FOX_DH) for h in heads]
    nt = (((1,), (1,)), ((), ()))
    causal = (lax.broadcasted_iota(jnp.int32, (tq, tq), 0)
              >= lax.broadcasted_iota(jnp.int32, (tq, tq), 1))
    kc = [kc_ref[0, 0, pl.ds(h, past, stride=FOX_HEADS), :].astype(BF16) for h in heads]
    s_c = [lax.dot_general(q_ref[:, sl], x, nt, preferred_element_type=F32) + nb[h:h + 1, :past]
           for h, sl, x in zip(heads, sls, kc)]
    s_n = [jnp.where(causal,
                     lax.dot_general(q_ref[:, sl], kn_ref[:, sl], nt,
                                     preferred_element_type=F32) + nb[h:h + 1, past:], NEG_BIG)
           for h, sl in zip(heads, sls)]
    m = [jnp.maximum(jnp.max(a, axis=-1, keepdims=True), jnp.max(b, axis=-1, keepdims=True))
         for a, b in zip(s_c, s_n)]
    p_c = [jnp.exp2(a - mm) for a, mm in zip(s_c, m)]
    p_n = [jnp.exp2(b - mm) for b, mm in zip(s_n, m)]
    l = [jnp.sum(a, axis=-1, keepdims=True) + jnp.sum(b, axis=-1, keepdims=True)
         for a, b in zip(p_c, p_n)]
    vc = [vc_ref[0, 0, pl.ds(h, past, stride=FOX_HEADS), :].astype(BF16) for h in heads]
    o = [jnp.dot(a.astype(BF16), x, preferred_element_type=F32)
         + jnp.dot(b.astype(BF16), vn_ref[:, sl], preferred_element_type=F32)
         for a, b, x, sl in zip(p_c, p_n, vc, sls)]
    for h, sl in zip(heads, sls):
        o_ref[:, sl] = (o[h] / l[h]).astype(o_ref.dtype)


def _attn_sample(q, k_new, v_new, k_cache, v_cache, layer, nbias, nseq):
    t, w = q.shape
    tq = t // nseq
    past = k_cache.shape[2] // FOX_HEADS
    row = pl.BlockSpec((tq, w), lambda b: (b, 0))
    cache = pl.BlockSpec((1, 1) + k_cache.shape[2:], lambda b: (layer, b, 0, 0))
    return pl.pallas_call(
        _attn_sample_kernel,
        grid=(nseq,),
        in_specs=[row, cache, cache, row, row,
                  pl.BlockSpec((1, FOX_HEADS, past + tq), lambda b: (b, 0, 0))],
        out_specs=row,
        out_shape=jax.ShapeDtypeStruct((t, w), BF16),
        compiler_params=_params(("parallel",)),
        name="attn_sample",
    )(q, k_cache, v_cache, k_new, v_new, nbias)


def _pad_lanes(w):
    return jnp.pad(w, ((0, 0), (0, LANES - w.shape[1])))


def _row(v, width=None):
    v = v.reshape(1, -1).astype(F32)
    if width is not None:
        v = jnp.pad(v, ((0, 0), (0, width - v.shape[1])))
    return v


def _prep_weights(norm_mix_g, norm_ffn_g, w_in_even, gdn_conv_w, gdn_a_log, gdn_dt_bias,
                  gdn_norm_g, sconv_w, w_out_even, w_in_odd, fox_b_f, fox_q_norm_g,
                  fox_k_norm_g, w_out_odd, ffn_w_gate, ffn_w_up, ffn_w_down):
    depth = norm_mix_g.shape[0]
    nqkv = gdn_conv_w.shape[2]
    nz = GDN_HEADS * GDN_DV
    nsc = sconv_w.shape[2]
    o1, o2 = nqkv, nqkv + nz
    o4 = o2 + 2 * GDN_HEADS
    o5, o6 = o4 + nsc, o4 + 2 * nsc
    fw = FOX_HEADS * FOX_DH
    shared = dict(g_ffn=norm_ffn_g.astype(F32).reshape(depth, 1, -1),
                  wg=ffn_w_gate.astype(BF16), wu=ffn_w_up.astype(BF16),
                  wd=ffn_w_down.astype(BF16),
                  wout_even=w_out_even.astype(BF16), wout_odd=w_out_odd.astype(BF16))
    layers = []
    for layer in range(depth):
        i = layer // 2
        p = dict(g_mix=_row(norm_mix_g[layer]))
        if layer % 2 == 0:
            w = w_in_even[i]
            p.update(wqkv=w[:, :o1].astype(BF16), wz=w[:, o1:o2].astype(BF16),
                     wab=_pad_lanes(w[:, o2:o4]).astype(BF16),
                     wgb=w[:, o4:o5].astype(BF16), wgc=w[:, o5:o6].astype(BF16),
                     whin=w[:, o6:].astype(BF16),
                     convw=gdn_conv_w[i].astype(F32), alog=_row(gdn_a_log[i], LANES),
                     dtb=_row(gdn_dt_bias[i], LANES), ng=_row(gdn_norm_g[i]),
                     scw=sconv_w[i].astype(F32))
        else:
            w = w_in_odd[i]
            p.update(wq=w[:, :fw].astype(BF16), wk=w[:, fw:2 * fw].astype(BF16),
                     wv=w[:, 2 * fw:3 * fw].astype(BF16),
                     wf=_pad_lanes(w[:, 3 * fw:]).astype(BF16),
                     bf=_row(fox_b_f[i], LANES), qg=_row(fox_q_norm_g[i]),
                     kg=_row(fox_k_norm_g[i]))
        layers.append(p)
    return layers, shared


class _Trunk:
    def __init__(self, x, gdn_conv, gdn_s, sconv, fox_k, fox_v, fox_logf):
        self.nseq, self.seq, self.d = x.shape
        self.x = x.reshape(self.nseq * self.seq, self.d)
        self.has_past = fox_k is not None
        if self.has_past:
            fox_k = fox_k.reshape(fox_k.shape[:2] + (-1, FOX_DH))
            fox_v = fox_v.reshape(fox_v.shape[:2] + (-1, FOX_DH))
        self.gdn_conv, self.gdn_s, self.sconv = gdn_conv, gdn_s, sconv
        self.fox_k, self.fox_v, self.fox_logf = fox_k, fox_v, fox_logf
        self.new_lf, self.new_conv, self.new_sc = [], [], []
        self.k_all = self.v_all = self.s_all = None

    def mixer(self, layer, p, nlayers):
        i = layer // 2
        nseq, seq = self.nseq, self.seq
        if layer % 2 == 0:
            if self.has_past:
                conv0, s_in, li, sc0 = self.gdn_conv[i], self.gdn_s, i, self.sconv[i]
            else:
                conv0 = jnp.zeros((nseq, GDN_CONV - 1, p["wqkv"].shape[1]), F32)
                s_in, li = jnp.zeros((1, nseq, GDN_HEADS, GDN_DK, GDN_DV), F32), 0
                sc0 = jnp.zeros((nseq, SC_CONV - 1, p["wgb"].shape[1]), F32)
            y, z, ab, mix_b, conv_fin, sc_fin = _in_even(
                self.x, p["g_mix"], p["wqkv"], p["wz"], p["wab"], p["wgb"], p["wgc"], p["whin"],
                conv0, sc0, p["convw"], p["scw"], nseq)
            mix_a, self.s_all = _gdn(y, z, ab, s_in, li, p["alog"], p["dtb"], p["ng"], i,
                                     (nlayers + 1) // 2, self.s_all)
            self.new_conv.append(conv_fin)
            self.new_sc.append(sc_fin)
            return mix_a, mix_b
        q, self.k_all, kb, self.v_all, vb, lf = _in_odd(
            self.x, p["g_mix"], p["wq"], p["wk"], p["wv"], p["wf"], p["bf"], p["qg"], p["kg"], i,
            nlayers // 2, self.k_all, self.v_all)
        lf = lf.reshape(nseq, seq, FOX_HEADS)
        self.new_lf.append(lf)
        if self.has_past:
            lf_all = jnp.concatenate([self.fox_logf[i].astype(F32), lf], axis=1)
            nbias = _key_bias(lf_all.transpose(0, 2, 1), ATTN_TILE)
            return _attn_sample(q, kb, vb, self.fox_k, self.fox_v, i, nbias, nseq), None
        nbias = _key_bias(lf.transpose(0, 2, 1), ATTN_TILE)
        return _attn_prompt(q, kb, vb, nbias, nseq), None

    def outputs(self, nlayers):
        kv_shape = (nlayers // 2, self.nseq, self.seq, FOX_HEADS, FOX_DH)
        return (self.x.reshape(self.nseq, self.seq, self.d), self.k_all.reshape(kv_shape),
                self.v_all.reshape(kv_shape), jnp.stack(self.new_lf), self.s_all,
                jnp.stack(self.new_conv), jnp.stack(self.new_sc))


def kernel(x_prompt, x_sample, cache_fox_k, cache_fox_v, cache_fox_logf, state_gdn_S,
           state_gdn_conv, state_sconv, norm_mix_g, norm_ffn_g, w_in_even, gdn_conv_w,
           gdn_a_log, gdn_dt_bias, gdn_norm_g, sconv_w, w_out_even, w_in_odd, fox_b_f,
           fox_q_norm_g, fox_k_norm_g, w_out_odd, ffn_w_gate, ffn_w_up, ffn_w_down):
    layers, shared = _prep_weights(
        norm_mix_g, norm_ffn_g, w_in_even, gdn_conv_w, gdn_a_log, gdn_dt_bias, gdn_norm_g,
        sconv_w, w_out_even, w_in_odd, fox_b_f, fox_q_norm_g, fox_k_norm_g, w_out_odd,
        ffn_w_gate, ffn_w_up, ffn_w_down)
    trunks = (_Trunk(x_prompt, None, None, None, None, None, None),
              _Trunk(x_sample, state_gdn_conv, state_gdn_S, state_sconv, cache_fox_k, cache_fox_v,
                     cache_fox_logf))
    nlayers = len(layers)
    for layer, p in enumerate(layers):
        mixes = tuple(t.mixer(layer, p, nlayers) for t in trunks)
        wout = shared["wout_even"] if layer % 2 == 0 else shared["wout_odd"]
        xs = _out_ffn(tuple(t.x for t in trunks), mixes, wout, layer // 2, shared["g_ffn"],
                      shared["wg"], shared["wu"], shared["wd"], layer)
        for t, x in zip(trunks, xs):
            t.x = x
    (y_prompt, p_fox_k, p_fox_v, p_fox_logf, p_gdn_s, p_gdn_conv, p_sconv) = trunks[0].outputs(nlayers)
    (y_sample, s_fox_k, s_fox_v, s_fox_logf, s_gdn_s, s_gdn_conv, s_sconv) = trunks[1].outputs(nlayers)
    return (y_prompt, y_sample, p_fox_k, p_fox_v, p_fox_logf, p_gdn_s, p_gdn_conv, p_sconv,
            s_fox_k, s_fox_v, s_fox_logf, s_gdn_s, s_gdn_conv, s_sconv)
```

```python
import functools

import jax
import jax.numpy as jnp
from jax import lax
from jax.experimental import pallas as pl
from jax.experimental.pallas import tpu as pltpu

F32 = jnp.float32
BF16 = jnp.bfloat16

EPS = 1e-6
CHUNK = 64
GDN_HEADS = 4
GDN_DK = 128
GDN_DV = 128
GDN_CONV = 4
SC_CONV = 3
FOX_HEADS = 8
FOX_DH = 128
LANES = 128
SUBLANES = 8
INV_BLOCK = 16
LOG2E = 1.4426950408889634
NEG_BIG = -1e30
VMEM_LIMIT_BYTES = 56 * 1024 * 1024

TOKEN_TILE = 512
FFN_TOKEN_TILE = 512
MXU_COLS = 256
ATTN_TILE = 256
ATTN_Q_BLOCKS = 4
KEY_BIAS_TILE = 1024
GDN_BLOCKS_PER_STEP = 8


def _params(semantics):
    return pltpu.CompilerParams(dimension_semantics=semantics,
                                vmem_limit_bytes=VMEM_LIMIT_BYTES)


def _mm(a, b):
    return jnp.dot(a.astype(BF16), b.astype(BF16), preferred_element_type=F32)


def _mm_nt(a, b):
    return lax.dot_general(a.astype(BF16), b.astype(BF16), (((1,), (1,)), ((), ())),
                           preferred_element_type=F32)


def _split3(x):
    hi = x.astype(BF16)
    r1 = x - hi.astype(F32)
    mid = r1.astype(BF16)
    lo = (r1 - mid.astype(F32)).astype(BF16)
    return hi, mid, lo


def _rms_norm(x, g):
    return x * lax.rsqrt(jnp.mean(x * x, axis=-1, keepdims=True) + EPS) * g


def _softplus(x):
    return jnp.maximum(x, 0.0) + jnp.log1p(jnp.exp(-jnp.abs(x)))


def _sigmoid(x):
    return 1.0 / (1.0 + jnp.exp(-x))


def _silu(x):
    return x * _sigmoid(x)


def _in_even_kernel(ns, rows, x_ref, g_ref, wqkv_ref, wz_ref, wab_ref, wgb_ref, wgc_ref,
                    whin_ref, conv0_ref, sc0_ref, convw_ref, scw_ref,
                    y_ref, z_ref, ab_ref, ob_ref, conv_ref, sc_ref, xslab, uslab):
    top = SUBLANES
    kc, ks = GDN_CONV - 1, SC_CONV - 1

    @pl.when(pl.program_id(1) == 0)
    def _():
        conv_ref[...] = conv0_ref[...]
        sc_ref[...] = sc0_ref[...]

    x = x_ref[...].reshape(ns * rows, x_ref.shape[2])
    h = _rms_norm(x, g_ref[...]).astype(BF16)
    cw = convw_ref[...]
    sw = scw_ref[...]

    def causal_conv(t, c0, slab, tail_ref, w, taps, s):
        for j in range(t.shape[1] // LANES):
            cs = slice(c0 + j * LANES, c0 + (j + 1) * LANES)
            k = cs.start // LANES
            ts = t[s * rows:(s + 1) * rows, j * LANES:(j + 1) * LANES]
            slab[s, k, top - taps:top, :] = tail_ref[s, :, cs]
            slab[s, k, top:top + rows, :] = ts
            tail_ref[s, :, cs] = ts[rows - taps:rows]
            acc = ts * w[taps:taps + 1, cs]
            for d in range(1, taps + 1):
                acc = acc + slab[s, k, top - d:top - d + rows, :] * w[taps - d:taps - d + 1, cs]
            yield cs, j, acc

    def dot(w_ref, cols):
        return jnp.dot(h, w_ref[:, cols], preferred_element_type=F32)

    jobs = []
    for c in range(0, wqkv_ref.shape[1], MXU_COLS):
        def conv_silu(t, c=c):
            for s in range(ns):
                for cs, _, acc in causal_conv(t, c, xslab, conv_ref, cw, kc, s):
                    y_ref[s, :, cs] = _silu(acc)
        jobs.append((lambda c=c: dot(wqkv_ref, slice(c, c + MXU_COLS)), conv_silu))
    for c in range(0, wgb_ref.shape[1], MXU_COLS):
        def gated_conv(t, c=c):
            gate_c, h_in, gate_b = t
            u = gate_c * h_in
            for s in range(ns):
                for cs, j, acc in causal_conv(u, c, uslab, sc_ref, sw, ks, s):
                    gb = gate_b[s * rows:(s + 1) * rows, j * LANES:(j + 1) * LANES]
                    ob_ref[s, :, cs] = (gb * acc).astype(ob_ref.dtype)
        cols = slice(c, c + MXU_COLS)
        jobs.append((lambda cols=cols: (dot(wgc_ref, cols), dot(whin_ref, cols), dot(wgb_ref, cols)),
                     gated_conv))

    def store_z(t):
        z_ref[...] = t.astype(BF16).reshape(z_ref.shape)

    def store_ab(t):
        ab_ref[...] = t.reshape(ab_ref.shape)
    jobs.append((lambda: jnp.dot(h, wz_ref[...], preferred_element_type=F32), store_z))
    jobs.append((lambda: jnp.dot(h, wab_ref[...], preferred_element_type=F32), store_ab))

    nxt = jobs[0][0]()
    for n, (_, post) in enumerate(jobs):
        cur = nxt
        if n + 1 < len(jobs):
            nxt = jobs[n + 1][0]()
        post(cur)


def _stream_blocking(nseq, seq, rows_per_step):
    rows = min(seq, rows_per_step)
    ns = max(1, min(nseq, rows_per_step // rows))
    return ns, rows


def _in_even(x, g, wqkv, wz, wab, wgb, wgc, whin, conv0, sc0, convw, scw, nseq):
    t, d = x.shape
    seq = t // nseq
    ns, rows = _stream_blocking(nseq, seq, TOKEN_TILE)
    nqkv, nz, nsc = wqkv.shape[1], wz.shape[1], wgb.shape[1]
    row = lambda n: pl.BlockSpec((ns, rows, n), lambda b, c: (b, c, 0))
    seq3 = lambda a: pl.BlockSpec((ns,) + a.shape[1:], lambda b, c: (b, 0, 0))
    full = lambda a: pl.BlockSpec(a.shape, lambda b, c: (0, 0))
    y, z, ab, ob, conv_fin, sc_fin = pl.pallas_call(
        functools.partial(_in_even_kernel, ns, rows),
        grid=(nseq // ns, seq // rows),
        in_specs=[row(d), full(g), full(wqkv), full(wz), full(wab), full(wgb), full(wgc),
                  full(whin), seq3(conv0), seq3(sc0), full(convw), full(scw)],
        out_specs=[row(nqkv), row(nz), row(LANES), row(nsc), seq3(conv0), seq3(sc0)],
        out_shape=[jax.ShapeDtypeStruct((nseq, seq, nqkv), F32),
                   jax.ShapeDtypeStruct((nseq, seq, nz), BF16),
                   jax.ShapeDtypeStruct((nseq, seq, LANES), F32),
                   jax.ShapeDtypeStruct((nseq, seq, nsc), BF16),
                   jax.ShapeDtypeStruct(conv0.shape, F32),
                   jax.ShapeDtypeStruct(sc0.shape, F32)],
        scratch_shapes=[pltpu.VMEM((ns, nqkv // LANES, SUBLANES + rows, LANES), F32),
                        pltpu.VMEM((ns, nsc // LANES, SUBLANES + rows, LANES), F32)],
        compiler_params=_params(("parallel", "arbitrary")),
        name="in_even",
    )(x.reshape(nseq, seq, d), g, wqkv, wz, wab, wgb, wgc, whin, conv0, sc0, convw, scw)
    return y, z, ab, ob.reshape(t, nsc), conv_fin, sc_fin


def _unit_lower_inverse_many(a_list):
    c = a_list[0].shape[0]
    ii = lax.broadcasted_iota(jnp.int32, (c, c), 0)
    jj = lax.broadcasted_iota(jnp.int32, (c, c), 1)
    eye = (ii == jj).astype(F32)
    same_block = (ii // INV_BLOCK) == (jj // INV_BLOCK)
    a0 = [jnp.where(same_block, a, 0.0) for a in a_list]
    a1 = [a - z for a, z in zip(a_list, a0)]
    d_inv = [eye - z for z in a0]
    power = a0
    for _ in range(INV_BLOCK.bit_length() - 2):
        power = [_mm(p, p) for p in power]
        d_inv = [d + _mm(d, p) for d, p in zip(d_inv, power)]
    n = [_mm(d, r) for d, r in zip(d_inv, a1)]
    m = [eye - x for x in n]
    power = n
    for _ in range(max((c // INV_BLOCK).bit_length() - 2, 0)):
        power = [_mm(p, p) for p in power]
        m = [x + _mm(x, p) for x, p in zip(m, power)]
    return [_mm(x, d) for x, d in zip(m, d_inv)]


def _gdn_kernel(ns, nc, y_ref, z_ref, ab_ref, s0_ref, alog_ref, dtb_ref, ng_ref, *rest):
    out_ref, s_ref = rest[-2:]
    ch = CHUNK
    nk = GDN_HEADS * GDN_DK

    @pl.when(pl.program_id(1) == 0)
    def _():
        s_ref[...] = s0_ref[...]

    alog, dtb, ng = alog_ref[...], dtb_ref[...], ng_ref[...]
    ii = lax.broadcasted_iota(jnp.int32, (ch, ch), 0)
    jj = lax.broadcasted_iota(jnp.int32, (ch, ch), 1)
    incl = ii >= jj
    strict = ii > jj
    tri = incl.astype(BF16)

    blocks = [(s, k) for s in range(ns) for k in range(nc)]
    chains = [(b, h) for b in range(len(blocks)) for h in range(GDN_HEADS)]

    ys, gcums, gcum_ts, betas = [], [], [], []
    for s, k in blocks:
        ys.append(y_ref[s, k * ch:(k + 1) * ch, :])
        ab = ab_ref[s, k * ch:(k + 1) * ch, :]
        g = -jnp.exp(alog) * _softplus(ab + dtb)
        betas.append(_sigmoid(ab))
        g_hi, g_mid, g_lo = _split3(g)
        gcum = (jnp.dot(tri, g_hi, preferred_element_type=F32)
                + jnp.dot(tri, g_mid, preferred_element_type=F32)
                + jnp.dot(tri, g_lo, preferred_element_type=F32))
        gcums.append(gcum)
        gcum_ts.append(gcum.T)

    def l2n(x):
        return x * lax.rsqrt(jnp.sum(x * x, axis=-1, keepdims=True) + EPS)

    q = [l2n(ys[b][:, h * GDN_DK:(h + 1) * GDN_DK]) * (GDN_DK ** -0.5) for b, h in chains]
    kk = [l2n(ys[b][:, nk + h * GDN_DK:nk + (h + 1) * GDN_DK]) for b, h in chains]
    v = [ys[b][:, 2 * nk + h * GDN_DV:2 * nk + (h + 1) * GDN_DV] for b, h in chains]
    gc = [gcums[b][:, h:h + 1] for b, h in chains]
    gr = [gcum_ts[b][h:h + 1, :] for b, h in chains]
    g_last = [gcums[b][ch - 1:ch, h:h + 1] for b, h in chains]
    bh = [betas[b][:, GDN_HEADS + h:GDN_HEADS + h + 1] for b, h in chains]
    decay = [jnp.exp(jnp.where(incl, c_ - r_, -jnp.inf)) for c_, r_ in zip(gc, gr)]
    kb = [x * b_ for x, b_ in zip(kk, bh)]
    a = [_mm_nt(x, y_) * jnp.where(strict, d, 0.0) for x, y_, d in zip(kb, kk, decay)]
    t_inv = _unit_lower_inverse_many(a)
    eg = [jnp.exp(c_) for c_ in gc]
    rhs = [jnp.concatenate([v_ * b_, x * e], axis=-1) for v_, b_, x, e in zip(v, bh, kb, eg)]
    sol = [_mm(t, r).astype(BF16) for t, r in zip(t_inv, rhs)]
    p = [(_mm_nt(x, y_) * d).astype(BF16) for x, y_, d in zip(q, kk, decay)]
    kd = [(x * jnp.exp(l - c_)).astype(BF16) for x, l, c_ in zip(kk, g_last, gc)]
    ksol = [lax.dot_general(x, y_, (((0,), (0,)), ((), ())), preferred_element_type=F32)
            for x, y_ in zip(kd, sol)]
    psol = [jnp.dot(x, y_, preferred_element_type=F32) for x, y_ in zip(p, sol)]
    lhs = [jnp.concatenate([-ks_[:, GDN_DV:], x * e - ps_[:, GDN_DV:]], axis=0).astype(BF16)
           for ks_, x, e, ps_ in zip(ksol, q, eg, psol)]
    sdecay = [jnp.exp(l) for l in g_last]

    state = [[s_ref[s, h] for h in range(GDN_HEADS)] for s in range(ns)]
    for k in range(nc):
        idx = [(s, h, (s * nc + k) * GDN_HEADS + h) for s in range(ns) for h in range(GDN_HEADS)]
        r = [jnp.dot(lhs[c], state[s][h].astype(BF16), preferred_element_type=F32)
             for s, h, c in idx]
        for (s, h, c), r_ in zip(idx, r):
            state[s][h] = state[s][h] * sdecay[c] + r_[:GDN_DK] + ksol[c][:, :GDN_DV]
        for (s, h, c), r_ in zip(idx, r):
            o = r_[GDN_DK:] + psol[c][:, :GDN_DV]
            zh = z_ref[s, k * ch:(k + 1) * ch, h * GDN_DV:(h + 1) * GDN_DV].astype(F32)
            o = _rms_norm(o, ng) * _silu(zh)
            out_ref[s, k * ch:(k + 1) * ch, h * GDN_DV:(h + 1) * GDN_DV] = o.astype(out_ref.dtype)
    for s in range(ns):
        for h in range(GDN_HEADS):
            s_ref[s, h] = state[s][h]


def _gdn(y, z, ab, s_in, li, alog, dtb, ng, slot, nslots, s_all):
    nseq, seq, nqkv = y.shape
    nz = z.shape[2]
    nchunk = seq // CHUNK
    ns = min(nseq, GDN_BLOCKS_PER_STEP)
    nc = max(1, min(nchunk, GDN_BLOCKS_PER_STEP // ns))
    rows = nc * CHUNK
    row = lambda n: pl.BlockSpec((ns, rows, n), lambda b, c: (b, c, 0))
    state = lambda layer: pl.BlockSpec((None, ns) + s_in.shape[2:],
                                       lambda b, c: (layer, b, 0, 0, 0))
    full = lambda a: pl.BlockSpec(a.shape, lambda b, c: (0, 0))
    args = [y, z, ab, s_in, alog, dtb, ng]
    in_specs = [row(nqkv), row(nz), row(LANES), state(li), full(alog), full(dtb), full(ng)]
    aliases = {}
    if s_all is not None:
        aliases = {len(args): 1}
        args.append(s_all)
        in_specs.append(pl.BlockSpec(memory_space=pl.ANY))
    o, s_all = pl.pallas_call(
        functools.partial(_gdn_kernel, ns, nc),
        grid=(nseq // ns, nchunk // nc),
        in_specs=in_specs,
        out_specs=[row(nz), state(slot)],
        out_shape=[jax.ShapeDtypeStruct((nseq, seq, nz), BF16),
                   jax.ShapeDtypeStruct((nslots,) + s_in.shape[1:], F32)],
        input_output_aliases=aliases,
        compiler_params=_params(("parallel", "arbitrary")),
        name="gdn",
    )(*args)
    return o.reshape(nseq * seq, nz), s_all


def _out_ffn_kernel(nfirst, *refs):
    (x0, a0, b0, x1, a1, b1, wa_ref, wb_ref, g_ref, wg_ref, wu_ref, wd_ref, o0, o1, act) = refs

    def tile(x_ref, ma_ref, mb_ref, o_ref):
        xn = (x_ref[...] + jnp.dot(ma_ref[...], wa_ref[0], preferred_element_type=F32)
              + jnp.dot(mb_ref[...], wb_ref[0], preferred_element_type=F32))
        h = _rms_norm(xn, g_ref[0]).astype(BF16)
        for c in range(0, wg_ref.shape[2], MXU_COLS):
            gate = jnp.dot(h, wg_ref[0, :, c:c + MXU_COLS], preferred_element_type=F32)
            up = jnp.dot(h, wu_ref[0, :, c:c + MXU_COLS], preferred_element_type=F32)
            act[:, c:c + MXU_COLS] = (_silu(gate) * up).astype(BF16)
        o_ref[...] = xn + jnp.dot(act[...], wd_ref[0], preferred_element_type=F32)

    @pl.when(pl.program_id(0) < nfirst)
    def _():
        tile(x0, a0, b0, o0)

    @pl.when(pl.program_id(0) >= nfirst)
    def _():
        tile(x1, a1, b1, o1)


def _out_ffn(xs, mixes, wout, iw, g, wg, wu, wd, layer):
    d = xs[0].shape[1]
    tm = min(FFN_TOKEN_TILE, xs[0].shape[0], xs[1].shape[0])
    n0, n1 = xs[0].shape[0] // tm, xs[1].shape[0] // tm
    dff = wg.shape[2]
    half = wout.shape[1] // 2
    imaps = [lambda i: jnp.minimum(i, n0 - 1), lambda i: jnp.maximum(i - n0, 0)]
    once = lambda shape, imap: pl.BlockSpec(shape, imap, pipeline_mode=pl.Buffered(1))
    args, in_specs, out_specs, out_shape = [], [], [], []
    for x, (ma, mb), im in zip(xs, mixes, imaps):
        col_b = 0
        if mb is None:
            mb, col_b = ma, 1
        args += [x, ma, mb]
        in_specs += [pl.BlockSpec((tm, d), lambda i, im=im: (im(i), 0)),
                     pl.BlockSpec((tm, half), lambda i, im=im: (im(i), 0)),
                     pl.BlockSpec((tm, half), lambda i, im=im, c=col_b: (im(i), c))]
        out_specs.append(pl.BlockSpec((tm, d), lambda i, im=im: (im(i), 0)))
        out_shape.append(jax.ShapeDtypeStruct(x.shape, F32))
    args += [wout, wout, g, wg, wu, wd]
    in_specs += [once((1, half, d), lambda i: (iw, 0, 0)),
                 once((1, half, d), lambda i: (iw, 1, 0)),
                 once((1, 1, d), lambda i: (layer, 0, 0)),
                 once((1, d, dff), lambda i: (layer, 0, 0)),
                 once((1, d, dff), lambda i: (layer, 0, 0)),
                 once((1, dff, d), lambda i: (layer, 0, 0))]
    return pl.pallas_call(
        functools.partial(_out_ffn_kernel, n0),
        grid=(n0 + n1,),
        in_specs=in_specs,
        out_specs=out_specs,
        out_shape=out_shape,
        scratch_shapes=[pltpu.VMEM((tm, dff), BF16)],
        compiler_params=_params(("arbitrary",)),
        name="out_ffn",
    )(*args)


def _in_odd_kernel(x_ref, g_ref, wq_ref, wk_ref, wv_ref, wf_ref, bf_ref, qg_ref, kg_ref, *rest):
    q_ref, k_ref, kb_ref, v_ref, vb_ref, lf_ref = rest[-6:]
    tm = x_ref.shape[0]
    h = _rms_norm(x_ref[...], g_ref[...]).astype(BF16)
    q = jnp.dot(h, wq_ref[...], preferred_element_type=F32)
    k = jnp.dot(h, wk_ref[...], preferred_element_type=F32)
    v = jnp.dot(h, wv_ref[...], preferred_element_type=F32)
    qg, kg = qg_ref[...], kg_ref[...]
    qscale = (FOX_DH ** -0.5) * LOG2E
    for hd in range(FOX_HEADS):
        sl = slice(hd * FOX_DH, (hd + 1) * FOX_DH)
        qn = _rms_norm(q[:, sl], qg)
        kn = _rms_norm(k[:, sl], kg)
        q_ref[:, sl] = (qn * qscale).astype(BF16)
        k_ref[0, pl.ds(hd, tm, stride=FOX_HEADS), :] = kn
        kb_ref[:, sl] = kn.astype(BF16)
        v_ref[0, pl.ds(hd, tm, stride=FOX_HEADS), :] = v[:, sl]
    vb_ref[...] = v.astype(BF16)
    f = jnp.dot(h, wf_ref[...], preferred_element_type=F32) + bf_ref[...]
    lf_ref[...] = (-_softplus(-f))[:, :FOX_HEADS]


def _in_odd(x, g, wq, wk, wv, wf, bf, qg, kg, slot, nslots, k_all, v_all):
    t, d = x.shape
    tm = min(TOKEN_TILE, t)
    w = wq.shape[1]
    row = lambda n: pl.BlockSpec((tm, n), lambda i: (i, 0))
    full = lambda a: pl.BlockSpec(a.shape, lambda i: (0, 0))
    stacked = pl.BlockSpec((1, tm * FOX_HEADS, FOX_DH), lambda i: (slot, i, 0))
    stacked_shape = jax.ShapeDtypeStruct((nslots, t * FOX_HEADS, FOX_DH), F32)
    args = [x, g, wq, wk, wv, wf, bf, qg, kg]
    in_specs = [row(d), full(g), full(wq), full(wk), full(wv), full(wf), full(bf),
                full(qg), full(kg)]
    aliases = {}
    if k_all is not None:
        aliases = {len(args): 1, len(args) + 1: 3}
        args += [k_all, v_all]
        in_specs += [pl.BlockSpec(memory_space=pl.ANY)] * 2
    return pl.pallas_call(
        _in_odd_kernel,
        grid=(t // tm,),
        in_specs=in_specs,
        out_specs=[row(w), stacked, row(w), stacked, row(w), row(FOX_HEADS)],
        out_shape=[jax.ShapeDtypeStruct((t, w), BF16), stacked_shape,
                   jax.ShapeDtypeStruct((t, w), BF16), stacked_shape,
                   jax.ShapeDtypeStruct((t, w), BF16),
                   jax.ShapeDtypeStruct((t, FOX_HEADS), F32)],
        input_output_aliases=aliases,
        compiler_params=_params(("parallel",)),
        name="in_odd",
    )(*args)


def _key_bias_kernel(lf_ref, o_ref, carry):
    @pl.when(pl.program_id(0) == 0)
    def _():
        carry[...] = jnp.zeros_like(carry)

    lf = lf_ref[...]
    tt = lf.shape[1]
    upper = (lax.broadcasted_iota(jnp.int32, (tt, tt), 0)
             <= lax.broadcasted_iota(jnp.int32, (tt, tt), 1)).astype(BF16)
    hi, mid, lo = _split3(lf)
    csum = (jnp.dot(hi, upper, preferred_element_type=F32)
            + jnp.dot(mid, upper, preferred_element_type=F32)
            + jnp.dot(lo, upper, preferred_element_type=F32)) + carry[:, 0:1]
    o_ref[...] = csum * (-LOG2E)
    carry[...] = jnp.broadcast_to(csum[:, tt - 1:tt], carry.shape)


def _key_bias(lf_t, tile):
    b, h, s = lf_t.shape
    tt = tile if s % tile == 0 else s
    return pl.pallas_call(
        _key_bias_kernel,
        grid=(s // tt,),
        in_specs=[pl.BlockSpec((b * h, tt), lambda j: (0, j))],
        out_specs=pl.BlockSpec((b * h, tt), lambda j: (0, j)),
        out_shape=jax.ShapeDtypeStruct((b * h, s), F32),
        scratch_shapes=[pltpu.VMEM((b * h, LANES), F32)],
        compiler_params=_params(("arbitrary",)),
        name="key_bias",
    )(lf_t.reshape(b * h, s)).reshape(b, h, s)


def _attn_prompt_kernel(ratio, qi_ref, kj_ref, q_ref, k_ref, v_ref, nb_ref, o_ref, m_ref, acc_ref):
    i, j = qi_ref[pl.program_id(1)], kj_ref[pl.program_id(1)]
    tk = k_ref.shape[0] // ratio
    heads = range(FOX_HEADS)
    sls = [slice(h * FOX_DH, (h + 1) * FOX_DH) for h in heads]

    @pl.when(j == 0)
    def _():
        m_ref[...] = jnp.full_like(m_ref, NEG_BIG)
        acc_ref[...] = jnp.zeros_like(acc_ref)

    def step(sub, row0, masked):
        nr = q_ref.shape[0] - row0
        keys = slice(sub * tk, (sub + 1) * tk)
        nb = nb_ref[0, :, keys]
        ones = jnp.ones((tk, LANES), BF16)
        reps = tk // LANES
        if masked:
            causal = (lax.broadcasted_iota(jnp.int32, (nr, tk), 0)
                      >= lax.broadcasted_iota(jnp.int32, (nr, tk), 1))
        rows = slice(row0, q_ref.shape[0])
        for h, sl in zip(heads, sls):
            s = lax.dot_general(q_ref[rows, sl], k_ref[keys, sl], (((1,), (1,)), ((), ())),
                                preferred_element_type=F32) + nb[h:h + 1, :]
            if masked:
                s = jnp.where(causal, s, NEG_BIG)
            m_prev = m_ref[h, rows, :]
            m_new = jnp.maximum(m_prev, jnp.max(s, axis=-1, keepdims=True))
            alpha = jnp.exp2(m_prev - m_new)
            p = jnp.exp2(s - jnp.concatenate([m_new] * reps, axis=-1)).astype(BF16)
            pv = jnp.dot(p, jnp.concatenate([v_ref[keys, sl], ones], axis=-1),
                         preferred_element_type=F32)
            acc_ref[h, rows, :] = jnp.concatenate([alpha, alpha], axis=-1) * acc_ref[h, rows, :] + pv
            m_ref[h, rows, :] = m_new

    @pl.when(j < i)
    def _():
        for sub in range(ratio):
            step(sub, 0, False)

    @pl.when(j == i)
    def _():
        for sub in range(ratio):
            step(sub, sub * tk, True)
        for h, sl in zip(heads, sls):
            acc = acc_ref[h]
            o_ref[:, sl] = (acc[:, :FOX_DH] / acc[:, FOX_DH:]).astype(o_ref.dtype)


def _attn_prompt(q, k, v, nbias, nseq):
    t, w = q.shape
    s = t // nseq
    tk = min(ATTN_TILE, s)
    ratio = ATTN_Q_BLOCKS if s % (ATTN_Q_BLOCKS * tk) == 0 else 1
    tq = ratio * tk
    nblk = s // tq
    pairs = [(i, j) for i in range(nblk) for j in range(i + 1)]
    qi = jnp.asarray([p[0] for p in pairs], jnp.int32)
    kj = jnp.asarray([p[1] for p in pairs], jnp.int32)
    q_spec = pl.BlockSpec((tq, w), lambda b, n, qi, kj: (b * nblk + qi[n], 0))
    kv_spec = pl.BlockSpec((tq, w), lambda b, n, qi, kj: (b * nblk + kj[n], 0))
    return pl.pallas_call(
        functools.partial(_attn_prompt_kernel, ratio),
        grid_spec=pltpu.PrefetchScalarGridSpec(
            num_scalar_prefetch=2,
            grid=(nseq, len(pairs)),
            in_specs=[q_spec, kv_spec, kv_spec,
                      pl.BlockSpec((1, FOX_HEADS, tq), lambda b, n, qi, kj: (b, 0, kj[n]))],
            out_specs=q_spec,
            scratch_shapes=[pltpu.VMEM((FOX_HEADS, tq, LANES), F32),
                            pltpu.VMEM((FOX_HEADS, tq, FOX_DH + LANES), F32)]),
        out_shape=jax.ShapeDtypeStruct((t, w), BF16),
        compiler_params=_params(("parallel", "arbitrary")),
        name="attn_prompt",
    )(qi, kj, q, k, v, nbias)


def _attn_sample_kernel(q_ref, kc_ref, vc_ref, kn_ref, vn_ref, nb_ref, o_ref):
    tq = q_ref.shape[0]
    past = kc_ref.shape[2] // FOX_HEADS
    nb = nb_ref[0]
    heads = range(FOX_HEADS)
    sls = [slice(h * FOX_DH, (h + 1) * FOX_DH) for h in heads]
    nt = (((1,), (1,)), ((), ()))
    causal = (lax.broadcasted_iota(jnp.int32, (tq, tq), 0)
              >= lax.broadcasted_iota(jnp.int32, (tq, tq), 1))
    kc = [kc_ref[0, 0, pl.ds(h, past, stride=FOX_HEADS), :].astype(BF16) for h in heads]
    s_c = [lax.dot_general(q_ref[:, sl], x, nt, preferred_element_type=F32) + nb[h:h + 1, :past]
           for h, sl, x in zip(heads, sls, kc)]
    s_n = [jnp.where(causal,
                     lax.dot_general(q_ref[:, sl], kn_ref[:, sl], nt,
                                     preferred_element_type=F32) + nb[h:h + 1, past:], NEG_BIG)
           for h, sl in zip(heads, sls)]
    m = [jnp.maximum(jnp.max(a, axis=-1, keepdims=True), jnp.max(b, axis=-1, keepdims=True))
         for a, b in zip(s_c, s_n)]
    p_c = [jnp.exp2(a - mm) for a, mm in zip(s_c, m)]
    p_n = [jnp.exp2(b - mm) for b, mm in zip(s_n, m)]
    l = [jnp.sum(a, axis=-1, keepdims=True) + jnp.sum(b, axis=-1, keepdims=True)
         for a, b in zip(p_c, p_n)]
    vc = [vc_ref[0, 0, pl.ds(h, past, stride=FOX_HEADS), :].astype(BF16) for h in heads]
    o = [jnp.dot(a.astype(BF16), x, preferred_element_type=F32)
         + jnp.dot(b.astype(BF16), vn_ref[:, sl], preferred_element_type=F32)
         for a, b, x, sl in zip(p_c, p_n, vc, sls)]
    for h, sl in zip(heads, sls):
        o_ref[:, sl] = (o[h] / l[h]).astype(o_ref.dtype)


def _attn_sample(q, k_new, v_new, k_cache, v_cache, layer, nbias, nseq):
    t, w = q.shape
    tq = t // nseq
    past = k_cache.shape[2] // FOX_HEADS
    row = pl.BlockSpec((tq, w), lambda b: (b, 0))
    cache = pl.BlockSpec((1, 1) + k_cache.shape[2:], lambda b: (layer, b, 0, 0))
    return pl.pallas_call(
        _attn_sample_kernel,
        grid=(nseq,),
        in_specs=[row, cache, cache, row, row,
                  pl.BlockSpec((1, FOX_HEADS, past + tq), lambda b: (b, 0, 0))],
        out_specs=row,
        out_shape=jax.ShapeDtypeStruct((t, w), BF16),
        compiler_params=_params(("parallel",)),
        name="attn_sample",
    )(q, k_cache, v_cache, k_new, v_new, nbias)


def _pad_lanes(w):
    return jnp.pad(w, ((0, 0), (0, LANES - w.shape[1])))


def _row(v, width=None):
    v = v.reshape(1, -1).astype(F32)
    if width is not None:
        v = jnp.pad(v, ((0, 0), (0, width - v.shape[1])))
    return v


def _prep_weights(norm_mix_g, norm_ffn_g, w_in_even, gdn_conv_w, gdn_a_log, gdn_dt_bias,
                  gdn_norm_g, sconv_w, w_out_even, w_in_odd, fox_b_f, fox_q_norm_g,
                  fox_k_norm_g, w_out_odd, ffn_w_gate, ffn_w_up, ffn_w_down):
    depth = norm_mix_g.shape[0]
    nqkv = gdn_conv_w.shape[2]
    nz = GDN_HEADS * GDN_DV
    nsc = sconv_w.shape[2]
    o1, o2 = nqkv, nqkv + nz
    o4 = o2 + 2 * GDN_HEADS
    o5, o6 = o4 + nsc, o4 + 2 * nsc
    fw = FOX_HEADS * FOX_DH
    shared = dict(g_ffn=norm_ffn_g.astype(F32).reshape(depth, 1, -1),
                  wg=ffn_w_gate.astype(BF16), wu=ffn_w_up.astype(BF16),
                  wd=ffn_w_down.astype(BF16),
                  wout_even=w_out_even.astype(BF16), wout_odd=w_out_odd.astype(BF16))
    layers = []
    for layer in range(depth):
        i = layer // 2
        p = dict(g_mix=_row(norm_mix_g[layer]))
        if layer % 2 == 0:
            w = w_in_even[i]
            p.update(wqkv=w[:, :o1].astype(BF16), wz=w[:, o1:o2].astype(BF16),
                     wab=_pad_lanes(w[:, o2:o4]).astype(BF16),
                     wgb=w[:, o4:o5].astype(BF16), wgc=w[:, o5:o6].astype(BF16),
                     whin=w[:, o6:].astype(BF16),
                     convw=gdn_conv_w[i].astype(F32), alog=_row(gdn_a_log[i], LANES),
                     dtb=_row(gdn_dt_bias[i], LANES), ng=_row(gdn_norm_g[i]),
                     scw=sconv_w[i].astype(F32))
        else:
            w = w_in_odd[i]
            p.update(wq=w[:, :fw].astype(BF16), wk=w[:, fw:2 * fw].astype(BF16),
                     wv=w[:, 2 * fw:3 * fw].astype(BF16),
                     wf=_pad_lanes(w[:, 3 * fw:]).astype(BF16),
                     bf=_row(fox_b_f[i], LANES), qg=_row(fox_q_norm_g[i]),
                     kg=_row(fox_k_norm_g[i]))
        layers.append(p)
    return layers, shared


class _Trunk:
    def __init__(self, x, gdn_conv, gdn_s, sconv, fox_k, fox_v, fox_logf):
        self.nseq, self.seq, self.d = x.shape
        self.x = x.reshape(self.nseq * self.seq, self.d)
        self.has_past = fox_k is not None
        if self.has_past:
            fox_k = fox_k.reshape(fox_k.shape[:2] + (-1, FOX_DH))
            fox_v = fox_v.reshape(fox_v.shape[:2] + (-1, FOX_DH))
        self.gdn_conv, self.gdn_s, self.sconv = gdn_conv, gdn_s, sconv
        self.fox_k, self.fox_v, self.fox_logf = fox_k, fox_v, fox_logf
        self.new_lf, self.new_conv, self.new_sc = [], [], []
        self.k_all = self.v_all = self.s_all = None

    def mixer(self, layer, p, nlayers):
        i = layer // 2
        nseq, seq = self.nseq, self.seq
        if layer % 2 == 0:
            if self.has_past:
                conv0, s_in, li, sc0 = self.gdn_conv[i], self.gdn_s, i, self.sconv[i]
            else:
                conv0 = jnp.zeros((nseq, GDN_CONV - 1, p["wqkv"].shape[1]), F32)
                s_in, li = jnp.zeros((1, nseq, GDN_HEADS, GDN_DK, GDN_DV), F32), 0
                sc0 = jnp.zeros((nseq, SC_CONV - 1, p["wgb"].shape[1]), F32)
            y, z, ab, mix_b, conv_fin, sc_fin = _in_even(
                self.x, p["g_mix"], p["wqkv"], p["wz"], p["wab"], p["wgb"], p["wgc"], p["whin"],
                conv0, sc0, p["convw"], p["scw"], nseq)
            mix_a, self.s_all = _gdn(y, z, ab, s_in, li, p["alog"], p["dtb"], p["ng"], i,
                                     (nlayers + 1) // 2, self.s_all)
            self.new_conv.append(conv_fin)
            self.new_sc.append(sc_fin)
            return mix_a, mix_b
        q, self.k_all, kb, self.v_all, vb, lf = _in_odd(
            self.x, p["g_mix"], p["wq"], p["wk"], p["wv"], p["wf"], p["bf"], p["qg"], p["kg"], i,
            nlayers // 2, self.k_all, self.v_all)
        lf = lf.reshape(nseq, seq, FOX_HEADS)
        self.new_lf.append(lf)
        if self.has_past:
            lf_all = jnp.concatenate([self.fox_logf[i].astype(F32), lf], axis=1)
            nbias = _key_bias(lf_all.transpose(0, 2, 1), KEY_BIAS_TILE)
            return _attn_sample(q, kb, vb, self.fox_k, self.fox_v, i, nbias, nseq), None
        nbias = _key_bias(lf.transpose(0, 2, 1), KEY_BIAS_TILE)
        return _attn_prompt(q, kb, vb, nbias, nseq), None

    def outputs(self, nlayers):
        kv_shape = (nlayers // 2, self.nseq, self.seq, FOX_HEADS, FOX_DH)
        return (self.x.reshape(self.nseq, self.seq, self.d), self.k_all.reshape(kv_shape),
                self.v_all.reshape(kv_shape), jnp.stack(self.new_lf), self.s_all,
                jnp.stack(self.new_conv), jnp.stack(self.new_sc))


def kernel(x_prompt, x_sample, cache_fox_k, cache_fox_v, cache_fox_logf, state_gdn_S,
           state_gdn_conv, state_sconv, norm_mix_g, norm_ffn_g, w_in_even, gdn_conv_w,
           gdn_a_log, gdn_dt_bias, gdn_norm_g, sconv_w, w_out_even, w_in_odd, fox_b_f,
           fox_q_norm_g, fox_k_norm_g, w_out_odd, ffn_w_gate, ffn_w_up, ffn_w_down):
    layers, shared = _prep_weights(
        norm_mix_g, norm_ffn_g, w_in_even, gdn_conv_w, gdn_a_log, gdn_dt_bias, gdn_norm_g,
        sconv_w, w_out_even, w_in_odd, fox_b_f, fox_q_norm_g, fox_k_norm_g, w_out_odd,
        ffn_w_gate, ffn_w_up, ffn_w_down)
    trunks = (_Trunk(x_prompt, None, None, None, None, None, None),
              _Trunk(x_sample, state_gdn_conv, state_gdn_S, state_sconv, cache_fox_k, cache_fox_v,
                     cache_fox_logf))
    nlayers = len(layers)
    for layer, p in enumerate(layers):
        mixes = tuple(t.mixer(layer, p, nlayers) for t in trunks)
        wout = shared["wout_even"] if layer % 2 == 0 else shared["wout_odd"]
        xs = _out_ffn(tuple(t.x for t in trunks), mixes, wout, layer // 2, shared["g_ffn"],
                      shared["wg"], shared["wu"], shared["wd"], layer)
        for t, x in zip(trunks, xs):
            t.x = x
    (y_prompt, p_fox_k, p_fox_v, p_fox_logf, p_gdn_s, p_gdn_conv, p_sconv) = trunks[0].outputs(nlayers)
    (y_sample, s_fox_k, s_fox_v, s_fox_logf, s_gdn_s, s_gdn_conv, s_sconv) = trunks[1].outputs(nlayers)
    return (y_prompt, y_sample, p_fox_k, p_fox_v, p_fox_logf, p_gdn_s, p_gdn_conv, p_sconv,
            s_fox_k, s_fox_v, s_fox_logf, s_gdn_s, s_gdn_conv, s_sconv)
```
